```python
import jax, jax.numpy as jnp
from jax import lax
import numpy as np

D_MODEL = 1024
BATCH = 4
SEQ = 8192
DEPTH = 4

MLA_HEADS = D_MODEL // 128
MLA_NOPE = 128
MLA_ROPE = 64
MLA_V = 128
MLA_Q_RANK = 256
MLA_KV_RANK = 128
ROPE_THETA = 10000.0
Q_BLOCK = 128
GLA_HEADS = 4
GLA_DK = D_MODEL // 2 // GLA_HEADS
GLA_DV = D_MODEL // GLA_HEADS
GLA_GATE_RANK = 16
GLA_GATE_NORM = 16.0
GLA_CHUNK = 64
CONV_WIDTH = 31
FFN_DIM = 2816
FFN_CONV_WIDTH = 3
N_MIXERS = 3
NORM_EPS = 1e-5
DEEPNORM_ALPHA = (2.0 * DEPTH) ** 0.25
DEEPNORM_BETA = (8.0 * DEPTH) ** -0.25

kernel_name = "hybrid_mla_gla_conformer_convffn_deepnorm"


def _layernorm(x, g, b):
    x32 = x.astype(jnp.float32)
    mu = jnp.mean(x32, axis=-1, keepdims=True)
    var = jnp.mean(jnp.square(x32 - mu), axis=-1, keepdims=True)
    y = (x32 - mu) * lax.rsqrt(var + NORM_EPS) * g.astype(jnp.float32) + b.astype(jnp.float32)
    return y.astype(x.dtype)


def _rmsnorm(x, g):
    x32 = x.astype(jnp.float32)
    y = x32 * lax.rsqrt(jnp.mean(jnp.square(x32), axis=-1, keepdims=True) + NORM_EPS) * g.astype(jnp.float32)
    return y.astype(x.dtype)


def _apply_rope(x, positions):
    half = MLA_ROPE // 2
    inv_freq = ROPE_THETA ** (-jnp.arange(half, dtype=jnp.float32) / half)
    ang = positions.astype(jnp.float32)[..., None] * inv_freq
    cos, sin = jnp.cos(ang).astype(x.dtype), jnp.sin(ang).astype(x.dtype)
    x1, x2 = x[..., :half], x[..., half:]
    return jnp.concatenate([x1 * cos - x2 * sin, x2 * cos + x1 * sin], axis=-1)


def _dwconv_causal(x, w, b):
    width, ch = w.shape
    y = lax.conv_general_dilated(
        x, w[:, None, :].astype(x.dtype), window_strides=(1,), padding=[(width - 1, 0)],
        dimension_numbers=('NWC', 'WIO', 'NWC'), feature_group_count=ch)
    return y + b


def _mla(x, positions, w_in, q_norm, kv_norm, w_uq, w_ukv, w_o):
    B, S, _ = x.shape
    H = MLA_HEADS
    c = x @ w_in
    c_q, c_kv, k_rope = jnp.split(c, [MLA_Q_RANK, MLA_Q_RANK + MLA_KV_RANK], axis=-1)
    q = (_rmsnorm(c_q, q_norm) @ w_uq).reshape(B, S, H, MLA_NOPE + MLA_ROPE)
    scale = (MLA_NOPE + MLA_ROPE) ** -0.5
    q_nope = q[..., :MLA_NOPE] * scale
    q_rope = _apply_rope(q[..., MLA_NOPE:], positions[:, :, None]) * scale
    k_rope = _apply_rope(k_rope, positions)
    kv = (_rmsnorm(c_kv, kv_norm) @ w_ukv).reshape(B, S, H, MLA_NOPE + MLA_V)
    k_nope, v = kv[..., :MLA_NOPE], kv[..., MLA_NOPE:]

    nb = S // Q_BLOCK
    qn_blocks = q_nope.reshape(B, nb, Q_BLOCK, H, MLA_NOPE).transpose(1, 0, 2, 3, 4)
    qr_blocks = q_rope.reshape(B, nb, Q_BLOCK, H, MLA_ROPE).transpose(1, 0, 2, 3, 4)
    key_idx = jnp.arange(S)

    def block(args):
        qn, qr, blk = args
        s = (jnp.einsum('bqhd,bkhd->bhqk', qn, k_nope)
             + jnp.einsum('bqhr,bkr->bhqk', qr, k_rope))
        q_idx = blk * Q_BLOCK + jnp.arange(Q_BLOCK)
        mask = key_idx[None, :] <= q_idx[:, None]
        s = jnp.where(mask, s.astype(jnp.float32), -jnp.inf)
        p = jax.nn.softmax(s, axis=-1).astype(v.dtype)
        return jnp.einsum('bhqk,bkhd->bqhd', p, v)

    o = lax.map(block, (qn_blocks, qr_blocks, jnp.arange(nb)))
    o = o.transpose(1, 0, 2, 3, 4).reshape(B, S, H * MLA_V)
    return o @ w_o


def _gla(x, w_in, w_a2, b_a, out_norm, w_o):
    B, S, _ = x.shape
    H, DK, DV, C = GLA_HEADS, GLA_DK, GLA_DV, GLA_CHUNK
    hk, hv = H * DK, H * DV
    proj = x @ w_in
    q, k, v, a_lr, r = jnp.split(proj, [hk, 2 * hk, 2 * hk + hv, 2 * hk + hv + GLA_GATE_RANK], axis=-1)
    log_a = jax.nn.log_sigmoid((a_lr @ w_a2 + b_a).astype(jnp.float32)) / GLA_GATE_NORM
    nc = S // C

    def to_chunks(t, d):
        return t.astype(jnp.float32).reshape(B, nc, C, H, d).transpose(1, 0, 3, 2, 4)

    qc = to_chunks(q * DK ** -0.5, DK)
    kc = to_chunks(k, DK)
    vc = to_chunks(v, DV)
    bc = jnp.cumsum(to_chunks(log_a, DK), axis=3)
    causal = jnp.tril(jnp.ones((C, C), dtype=bool))

    def step(state, inp):
        q_, k_, v_, b_ = inp
        o_inter = jnp.einsum('bhcd,bhde->bhce', q_ * jnp.exp(b_), state)
        diff = b_[:, :, :, None, :] - b_[:, :, None, :, :]
        decay = jnp.exp(jnp.where(causal[:, :, None], diff, -jnp.inf))
        attn = jnp.einsum('bhid,bhjd,bhijd->bhij', q_, k_, decay)
        o = o_inter + jnp.einsum('bhij,bhje->bhie', attn, v_)
        b_last = b_[:, :, -1:, :]
        k_dec = k_ * jnp.exp(b_last - b_)
        state = jnp.exp(b_last[:, :, 0, :, None]) * state + jnp.einsum('bhcd,bhce->bhde', k_dec, v_)
        return state, o

    state0 = jnp.zeros((B, H, DK, DV), jnp.float32)
    _, o = lax.scan(step, state0, (qc, kc, vc, bc))
    o = o.transpose(1, 0, 3, 2, 4).reshape(B, S, H, DV)
    o = _rmsnorm(o, out_norm).astype(x.dtype).reshape(B, S, hv)
    return (o * jax.nn.silu(r)) @ w_o


def _conformer_conv(x, w_in, b_in, dw, dw_b, ln_g, ln_b, w_o, b_o):
    h = x @ w_in + b_in
    a, g = jnp.split(h, 2, axis=-1)
    h = a * jax.nn.sigmoid(g)
    h = _dwconv_causal(h, dw, dw_b)
    h = jax.nn.silu(_layernorm(h, ln_g, ln_b))
    return h @ w_o + b_o


def _conv_ffn(x, w_in, conv, conv_b, w_out):
    u = _dwconv_causal(x @ w_in, conv, conv_b)
    g, val = jnp.split(u, 2, axis=-1)
    return (jax.nn.silu(g) * val) @ w_out


def _normal(key, shape, scale):
    return scale * jax.random.normal(key, shape, jnp.float32)


def _gain(key, n):
    return 1.0 + 0.02 * jax.random.normal(key, (n,), jnp.float32)


def _bias(key, n):
    return 0.02 * jax.random.normal(key, (n,), jnp.float32)


def _mla_params(key, p):
    ks = jax.random.split(key, 6)
    hv = MLA_HEADS * MLA_V
    return {
        p + "mla_w_in": _normal(ks[0], (D_MODEL, MLA_Q_RANK + MLA_KV_RANK + MLA_ROPE), D_MODEL ** -0.5),
        p + "mla_q_norm": _gain(ks[1], MLA_Q_RANK),
        p + "mla_kv_norm": _gain(ks[2], MLA_KV_RANK),
        p + "mla_w_uq": _normal(ks[3], (MLA_Q_RANK, MLA_HEADS * (MLA_NOPE + MLA_ROPE)), MLA_Q_RANK ** -0.5),
        p + "mla_w_ukv": _normal(ks[4], (MLA_KV_RANK, MLA_HEADS * (MLA_NOPE + MLA_V)), MLA_KV_RANK ** -0.5),
        p + "mla_w_o": _normal(ks[5], (hv, D_MODEL), DEEPNORM_BETA * hv ** -0.5),
    }


def _gla_params(key, p):
    ks = jax.random.split(key, 5)
    hk, hv = GLA_HEADS * GLA_DK, GLA_HEADS * GLA_DV
    return {
        p + "gla_w_in": _normal(ks[0], (D_MODEL, 2 * hk + hv + GLA_GATE_RANK + hv), D_MODEL ** -0.5),
        p + "gla_w_a2": _normal(ks[1], (GLA_GATE_RANK, hk), GLA_GATE_RANK ** -0.5),
        p + "gla_b_a": _bias(ks[2], hk),
        p + "gla_out_norm": _gain(ks[3], GLA_DV),
        p + "gla_w_o": _normal(ks[4], (hv, D_MODEL), DEEPNORM_BETA * hv ** -0.5),
    }


def _conv_params(key, p):
    ks = jax.random.split(key, 8)
    return {
        p + "conv_w_in": _normal(ks[0], (D_MODEL, 2 * D_MODEL), D_MODEL ** -0.5),
        p + "conv_b_in": _bias(ks[1], 2 * D_MODEL),
        p + "conv_dw": _normal(ks[2], (CONV_WIDTH, D_MODEL), CONV_WIDTH ** -0.5),
        p + "conv_dw_b": _bias(ks[3], D_MODEL),
        p + "conv_ln_g": _gain(ks[4], D_MODEL),
        p + "conv_ln_b": _bias(ks[5], D_MODEL),
        p + "conv_w_o": _normal(ks[6], (D_MODEL, D_MODEL), DEEPNORM_BETA * D_MODEL ** -0.5),
        p + "conv_b_o": _bias(ks[7], D_MODEL),
    }


def _layer_params(key, i):
    k_mix, k_ln1, k_ffn, k_ln2 = jax.random.split(key, 4)
    p = "l%d_" % i
    kind = i % N_MIXERS
    if kind == 0:
        d = _mla_params(k_mix, p)
    elif kind == 1:
        d = _gla_params(k_mix, p)
    else:
        d = _conv_params(k_mix, p)
    a, b = jax.random.split(k_ln1)
    d[p + "ln1_g"] = _gain(a, D_MODEL)
    d[p + "ln1_b"] = _bias(b, D_MODEL)
    kf = jax.random.split(k_ffn, 4)
    d[p + "ffn_w_in"] = _normal(kf[0], (D_MODEL, 2 * FFN_DIM), D_MODEL ** -0.5)
    d[p + "ffn_conv"] = _normal(kf[1], (FFN_CONV_WIDTH, 2 * FFN_DIM), FFN_CONV_WIDTH ** -0.5)
    d[p + "ffn_conv_b"] = _bias(kf[2], 2 * FFN_DIM)
    d[p + "ffn_w_out"] = _normal(kf[3], (FFN_DIM, D_MODEL), DEEPNORM_BETA * FFN_DIM ** -0.5)
    a, b = jax.random.split(k_ln2)
    d[p + "ln2_g"] = _gain(a, D_MODEL)
    d[p + "ln2_b"] = _bias(b, D_MODEL)
    return d


def setup_inputs(seed: int = 0) -> dict:
    key = jax.random.key(seed)
    keys = jax.random.split(key, DEPTH + 1)
    out = {
        "x": jax.random.normal(keys[0], (BATCH, SEQ, D_MODEL), jnp.float32),
        "positions": jnp.broadcast_to(jnp.arange(SEQ, dtype=jnp.int32)[None, :], (BATCH, SEQ)),
    }
    for i in range(DEPTH):
        out.update(_layer_params(keys[i + 1], i))
    return out


def reference(x, positions,
              l0_mla_w_in, l0_mla_q_norm, l0_mla_kv_norm, l0_mla_w_uq, l0_mla_w_ukv, l0_mla_w_o,
              l0_ln1_g, l0_ln1_b, l0_ffn_w_in, l0_ffn_conv, l0_ffn_conv_b, l0_ffn_w_out, l0_ln2_g, l0_ln2_b,
              l1_gla_w_in, l1_gla_w_a2, l1_gla_b_a, l1_gla_out_norm, l1_gla_w_o,
              l1_ln1_g, l1_ln1_b, l1_ffn_w_in, l1_ffn_conv, l1_ffn_conv_b, l1_ffn_w_out, l1_ln2_g, l1_ln2_b,
              l2_conv_w_in, l2_conv_b_in, l2_conv_dw, l2_conv_dw_b, l2_conv_ln_g, l2_conv_ln_b, l2_conv_w_o, l2_conv_b_o,
              l2_ln1_g, l2_ln1_b, l2_ffn_w_in, l2_ffn_conv, l2_ffn_conv_b, l2_ffn_w_out, l2_ln2_g, l2_ln2_b,
              l3_mla_w_in, l3_mla_q_norm, l3_mla_kv_norm, l3_mla_w_uq, l3_mla_w_ukv, l3_mla_w_o,
              l3_ln1_g, l3_ln1_b, l3_ffn_w_in, l3_ffn_conv, l3_ffn_conv_b, l3_ffn_w_out, l3_ln2_g, l3_ln2_b):
    mixer_args = [
        (l0_mla_w_in, l0_mla_q_norm, l0_mla_kv_norm, l0_mla_w_uq, l0_mla_w_ukv, l0_mla_w_o),
        (l1_gla_w_in, l1_gla_w_a2, l1_gla_b_a, l1_gla_out_norm, l1_gla_w_o),
        (l2_conv_w_in, l2_conv_b_in, l2_conv_dw, l2_conv_dw_b, l2_conv_ln_g, l2_conv_ln_b, l2_conv_w_o, l2_conv_b_o),
        (l3_mla_w_in, l3_mla_q_norm, l3_mla_kv_norm, l3_mla_w_uq, l3_mla_w_ukv, l3_mla_w_o),
    ]
    ln1_args = [(l0_ln1_g, l0_ln1_b), (l1_ln1_g, l1_ln1_b), (l2_ln1_g, l2_ln1_b), (l3_ln1_g, l3_ln1_b)]
    ffn_args = [
        (l0_ffn_w_in, l0_ffn_conv, l0_ffn_conv_b, l0_ffn_w_out),
        (l1_ffn_w_in, l1_ffn_conv, l1_ffn_conv_b, l1_ffn_w_out),
        (l2_ffn_w_in, l2_ffn_conv, l2_ffn_conv_b, l2_ffn_w_out),
        (l3_ffn_w_in, l3_ffn_conv, l3_ffn_conv_b, l3_ffn_w_out),
    ]
    ln2_args = [(l0_ln2_g, l0_ln2_b), (l1_ln2_g, l1_ln2_b), (l2_ln2_g, l2_ln2_b), (l3_ln2_g, l3_ln2_b)]

    for i in range(DEPTH):
        kind = i % N_MIXERS
        if kind == 0:
            h = _mla(x, positions, *mixer_args[i])
        elif kind == 1:
            h = _gla(x, *mixer_args[i])
        else:
            h = _conformer_conv(x, *mixer_args[i])
        x = _layernorm(DEEPNORM_ALPHA * x + h, *ln1_args[i])
        x = _layernorm(DEEPNORM_ALPHA * x + _conv_ffn(x, *ffn_args[i]), *ln2_args[i])
    return x
```

```python
import functools
import math

import numpy as np
import jax
import jax.numpy as jnp
from jax import lax
from jax.experimental import pallas as pl
from jax.experimental.pallas import tpu as pltpu

F32 = jnp.float32
BF16 = jnp.bfloat16

D_MODEL = 1024
DEPTH = 4
MLA_HEADS = 8
MLA_NOPE = 128
MLA_ROPE = 64
MLA_V = 128
MLA_Q_RANK = 256
MLA_KV_RANK = 128
MLA_QK_PAD = 256
ROPE_THETA = 10000.0
GLA_HEADS = 4
GLA_DK = 128
GLA_DV = 256
GLA_GATE_RANK = 16
GLA_GATE_NORM = 16.0
GLA_CHUNK = 256
CONV_WIDTH = 31
CONV_HALO = 32
FFN_DIM = 2816
FFN_CONV_WIDTH = 3
NORM_EPS = 1e-5
DEEPNORM_ALPHA = (2.0 * DEPTH) ** 0.25
SUBLANES = 8
LANES = 128
VMEM_LIMIT = 56 * 1024 * 1024


def _dot(a, b):
    return jnp.dot(a, b, preferred_element_type=F32)


def _dot_nt(a, b):
    return lax.dot_general(a, b, (((1,), (1,)), ((), ())), preferred_element_type=F32)


def _dot_tn(a, b):
    return lax.dot_general(a, b, (((0,), (0,)), ((), ())), preferred_element_type=F32)


def _layernorm(y, g, b):
    mu = jnp.mean(y, axis=-1, keepdims=True)
    d = y - mu
    var = jnp.mean(d * d, axis=-1, keepdims=True)
    return d * lax.rsqrt(var + NORM_EPS) * g + b


def _rmsnorm(y, g):
    return y * lax.rsqrt(jnp.mean(y * y, axis=-1, keepdims=True) + NORM_EPS) * g


def _sigmoid(x):
    return 1.0 / (1.0 + jnp.exp(-x))


def _const_spec(shape):
    nd = len(shape)
    return pl.BlockSpec(shape, lambda *_: (0,) * nd, pipeline_mode=pl.Buffered(1))


def _params(n_axes):
    return pltpu.CompilerParams(dimension_semantics=("arbitrary",) * n_axes, vmem_limit_bytes=VMEM_LIMIT)


def _row(v):
    return v.reshape(1, -1).astype(F32)


def _rope_table_kernel(pos_ref, freq_ref, cos_ref, sin_ref):
    ang = pos_ref[0] * freq_ref[...]
    cos_ref[0] = jnp.cos(ang)
    sin_ref[0] = jnp.sin(ang)


def _rope_table(positions, tm):
    B, S = positions.shape
    half = MLA_ROPE // 2
    inv_freq = ROPE_THETA ** (-jnp.arange(half, dtype=F32) / half)
    freq = jnp.tile(inv_freq, LANES // half).reshape(1, LANES)
    posf = positions.astype(F32).reshape(B, S, 1)
    out = jax.ShapeDtypeStruct((B, S, LANES), F32)
    return pl.pallas_call(
        _rope_table_kernel,
        out_shape=(out, out),
        grid=(B, S // tm),
        in_specs=[pl.BlockSpec((1, tm, 1), lambda b, s: (b, s, 0)), _const_spec((1, LANES))],
        out_specs=(pl.BlockSpec((1, tm, LANES), lambda b, s: (b, s, 0)),
                   pl.BlockSpec((1, tm, LANES), lambda b, s: (b, s, 0))),
        compiler_params=_params(2),
        name="rope_table",
    )(posf, freq)


def _mla_proj_kernel(x_ref, cos_ref, sin_ref, w_in_ref, qn_ref, kvn_ref, w_uq_ref, w_ukv_ref,
                     q_ref, k_ref, v_ref):
    H = MLA_HEADS
    x = x_ref[0].astype(BF16)
    c = _dot(x, w_in_ref[...])
    cos = cos_ref[0]
    sin = sin_ref[0]
    c_q = _rmsnorm(c[:, :MLA_Q_RANK], qn_ref[...]).astype(BF16)
    c_kv = _rmsnorm(c[:, MLA_Q_RANK:MLA_Q_RANK + MLA_KV_RANK], kvn_ref[...]).astype(BF16)
    o = MLA_Q_RANK + MLA_KV_RANK
    k_rope = (c[:, o:o + LANES] * cos + c[:, o + LANES:o + 2 * LANES] * sin).astype(BF16)
    scale = (MLA_NOPE + MLA_ROPE) ** -0.5
    q_all = _dot(c_q, w_uq_ref[...]) * scale
    kv = _dot(c_kv, w_ukv_ref[...])
    for h in range(H):
        q_ref[0, h, :, 0:LANES] = q_all[:, h * LANES:(h + 1) * LANES].astype(BF16)
        qr = (q_all[:, (H + h) * LANES:(H + h + 1) * LANES] * cos
              + q_all[:, (2 * H + h) * LANES:(2 * H + h + 1) * LANES] * sin)
        q_ref[0, h, :, LANES:2 * LANES] = qr.astype(BF16)
        k_ref[0, h, :, 0:LANES] = kv[:, 2 * h * LANES:(2 * h + 1) * LANES].astype(BF16)
        k_ref[0, h, :, LANES:2 * LANES] = k_rope
        v_ref[0, h] = kv[:, (2 * h + 1) * LANES:(2 * h + 2) * LANES].astype(BF16)


def _rot_half_cols(w):
    half = w.shape[1] // 2
    return jnp.concatenate([-w[:, half:], w[:, :half]], axis=1)


def _mla_proj(x, cos, sin, w_in, q_norm, kv_norm, w_uq, w_ukv, tm):
    B, S, D = x.shape
    H = MLA_HEADS
    o = MLA_Q_RANK + MLA_KV_RANK
    kr = w_in[:, o:]
    krot = _rot_half_cols(kr)
    w_in_ext = jnp.concatenate([w_in[:, :o], kr, kr, krot, krot], axis=1).astype(BF16)
    w_uq3 = w_uq.reshape(MLA_Q_RANK, H, MLA_NOPE + MLA_ROPE)
    nope = w_uq3[:, :, :MLA_NOPE].reshape(MLA_Q_RANK, H * MLA_NOPE)
    rope = w_uq3[:, :, MLA_NOPE:]
    rot = jnp.concatenate([-rope[:, :, MLA_ROPE // 2:], rope[:, :, :MLA_ROPE // 2]], axis=2)
    zpad = jnp.zeros((MLA_Q_RANK, H, LANES - MLA_ROPE), w_uq.dtype)
    rope_p = jnp.concatenate([rope, zpad], axis=2).reshape(MLA_Q_RANK, H * LANES)
    rot_p = jnp.concatenate([rot, zpad], axis=2).reshape(MLA_Q_RANK, H * LANES)
    w_uq_ext = jnp.concatenate([nope, rope_p, rot_p], axis=1).astype(BF16)
    qk = jax.ShapeDtypeStruct((B, H, S, MLA_QK_PAD), BF16)
    vv = jax.ShapeDtypeStruct((B, H, S, MLA_V), BF16)
    tok = lambda w: pl.BlockSpec((1, tm, w), lambda b, s: (b, s, 0))
    hd = lambda w: pl.BlockSpec((1, H, tm, w), lambda b, s: (b, 0, s, 0))
    return pl.pallas_call(
        _mla_proj_kernel,
        out_shape=(qk, qk, vv),
        grid=(B, S // tm),
        in_specs=[tok(D), tok(LANES), tok(LANES), _const_spec(w_in_ext.shape), _const_spec((1, MLA_Q_RANK)),
                  _const_spec((1, MLA_KV_RANK)), _const_spec(w_uq_ext.shape), _const_spec(w_ukv.shape)],
        out_specs=(hd(MLA_QK_PAD), hd(MLA_QK_PAD), hd(MLA_V)),
        compiler_params=_params(2),
        name="mla_proj",
    )(x, cos, sin, w_in_ext, _row(q_norm), _row(kv_norm), w_uq_ext, w_ukv.astype(BF16))


def _flash_kernel(q_ref, k_ref, v_ref, o_ref, *, blk):
    qi = pl.program_id(2)
    q = q_ref[0, 0]

    def step(j, carry, masked):
        m, l, acc = carry
        start = pl.multiple_of(j * blk, blk)
        k = k_ref[0, 0, pl.ds(start, blk), :]
        v = v_ref[0, 0, pl.ds(start, blk), :]
        s = _dot_nt(q, k)
        if masked:
            row = lax.broadcasted_iota(jnp.int32, (blk, blk), 0)
            col = lax.broadcasted_iota(jnp.int32, (blk, blk), 1)
            s = jnp.where(col <= row, s, -jnp.inf)
        m_new = jnp.maximum(m, jnp.max(s, axis=-1, keepdims=True))
        p = jnp.exp(s - m_new)
        alpha = jnp.exp(m - m_new)
        l = alpha * l + jnp.sum(p, axis=-1, keepdims=True)
        acc = alpha * acc + _dot(p.astype(BF16), v)
        return m_new, l, acc

    init = (jnp.full((blk, 1), -jnp.inf, F32), jnp.zeros((blk, 1), F32), jnp.zeros((blk, MLA_V), F32))
    carry = lax.fori_loop(0, qi, lambda j, c: step(j, c, False), init)
    m, l, acc = step(qi, carry, True)
    o_ref[0] = (acc / l).astype(o_ref.dtype)


def _flash(q, k, v, blk):
    B, H, S, _ = q.shape
    return pl.pallas_call(
        functools.partial(_flash_kernel, blk=blk),
        out_shape=jax.ShapeDtypeStruct((B, S, H * MLA_V), BF16),
        grid=(B, H, S // blk),
        in_specs=[pl.BlockSpec((1, 1, blk, MLA_QK_PAD), lambda b, h, i: (b, h, i, 0)),
                  pl.BlockSpec((1, 1, S, MLA_QK_PAD), lambda b, h, i: (b, h, 0, 0)),
                  pl.BlockSpec((1, 1, S, MLA_V), lambda b, h, i: (b, h, 0, 0))],
        out_specs=pl.BlockSpec((1, blk, MLA_V), lambda b, h, i: (b, i, h)),
        compiler_params=_params(3),
        name="mla_flash",
    )(q, k, v)


def _out_proj_kernel(a_ref, x_ref, w_ref, bias_ref, g_ref, b_ref, o_ref):
    h = _dot(a_ref[0], w_ref[...]) + bias_ref[...]
    o_ref[0] = _layernorm(DEEPNORM_ALPHA * x_ref[0] + h, g_ref[...], b_ref[...])


def _out_proj(a, x, w, bias, g, b, tm):
    B, S, D = x.shape
    K = a.shape[-1]
    return pl.pallas_call(
        _out_proj_kernel,
        out_shape=jax.ShapeDtypeStruct((B, S, D), F32),
        grid=(B, S // tm),
        in_specs=[pl.BlockSpec((1, tm, K), lambda b_, s: (b_, s, 0)),
                  pl.BlockSpec((1, tm, D), lambda b_, s: (b_, s, 0)),
                  _const_spec((K, D)), _const_spec((1, D)), _const_spec((1, D)), _const_spec((1, D))],
        out_specs=pl.BlockSpec((1, tm, D), lambda b_, s: (b_, s, 0)),
        compiler_params=_params(2),
        name="out_proj_ln",
    )(a, x, w.astype(BF16), _row(bias), _row(g), _row(b))


def _ffn_kernel(x_ref, w_in_ref, cw_ref, cb_ref, w_out_ref, g_ref, b_ref, o_ref, halo_ref, *, tm, fc):
    @pl.when(pl.program_id(1) == 0)
    def _():
        halo_ref[...] = jnp.zeros_like(halo_ref)

    x = x_ref[0]
    xb = x.astype(BF16)

    def conv(col0):
        u = _dot(xb, w_in_ref[:, col0:col0 + fc])
        ext = jnp.concatenate([halo_ref[:, col0:col0 + fc], u], axis=0)
        halo_ref[:, col0:col0 + fc] = u[tm - SUBLANES:, :]
        s1 = pltpu.roll(ext, 1, 0)[SUBLANES:, :]
        s2 = pltpu.roll(ext, 2, 0)[SUBLANES:, :]
        w = cw_ref[:, col0:col0 + fc]
        return w[2:3, :] * u + w[1:2, :] * s1 + w[0:1, :] * s2 + cb_ref[:, col0:col0 + fc]

    acc = jnp.zeros((tm, D_MODEL), F32)
    for j in range(FFN_DIM // fc):
        gate = conv(j * fc)
        val = conv(FFN_DIM + j * fc)
        act = (gate * _sigmoid(gate) * val).astype(BF16)
        acc = acc + _dot(act, w_out_ref[j * fc:(j + 1) * fc, :])
    o_ref[0] = _layernorm(DEEPNORM_ALPHA * x + acc, g_ref[...], b_ref[...])


def _ffn(x, w_in, conv_w, conv_b, w_out, g, b, tm, fc):
    B, S, D = x.shape
    return pl.pallas_call(
        functools.partial(_ffn_kernel, tm=tm, fc=fc),
        out_shape=jax.ShapeDtypeStruct((B, S, D), F32),
        grid=(B, S // tm),
        in_specs=[pl.BlockSpec((1, tm, D), lambda b_, s: (b_, s, 0)),
                  _const_spec((D, 2 * FFN_DIM)), _const_spec((FFN_CONV_WIDTH, 2 * FFN_DIM)),
                  _const_spec((1, 2 * FFN_DIM)), _const_spec((FFN_DIM, D)),
                  _const_spec((1, D)), _const_spec((1, D))],
        out_specs=pl.BlockSpec((1, tm, D), lambda b_, s: (b_, s, 0)),
        scratch_shapes=[pltpu.VMEM((SUBLANES, 2 * FFN_DIM), F32)],
        compiler_params=_params(2),
        name="conv_ffn_ln",
    )(x, w_in.astype(BF16), conv_w.astype(F32), _row(conv_b), w_out.astype(BF16), _row(g), _row(b))


def _gla_proj_kernel(x_ref, w_in_ref, w_a2_ref, b_a_ref, q_ref, k_ref, v_ref, la_ref, gate_ref):
    hk = GLA_HEADS * GLA_DK
    hv = GLA_HEADS * GLA_DV
    proj = _dot(x_ref[0].astype(BF16), w_in_ref[...])
    q_ref[0] = (proj[:, :hk] * GLA_DK ** -0.5).astype(BF16)
    k_ref[0] = proj[:, hk:2 * hk].astype(BF16)
    v_ref[0] = proj[:, 2 * hk:2 * hk + hv].astype(BF16)
    r = proj[:, 2 * hk + hv:2 * hk + 2 * hv]
    gate_ref[0] = (r * _sigmoid(r)).astype(BF16)
    a_lr = proj[:, 2 * hk + 2 * hv:].astype(BF16)
    z = _dot(a_lr, w_a2_ref[...]) + b_a_ref[...]
    la_ref[0] = (jnp.minimum(z, 0.0) - jnp.log(1.0 + jnp.exp(-jnp.abs(z)))) * (1.0 / GLA_GATE_NORM)


def _gla_proj(x, w_in, w_a2, b_a, tm):
    B, S, D = x.shape
    hk = GLA_HEADS * GLA_DK
    hv = GLA_HEADS * GLA_DV
    a0 = 2 * hk + hv
    a_cols = jnp.pad(w_in[:, a0:a0 + GLA_GATE_RANK], ((0, 0), (0, LANES - GLA_GATE_RANK)))
    w_ext = jnp.concatenate([w_in[:, :a0], w_in[:, a0 + GLA_GATE_RANK:], a_cols], axis=1).astype(BF16)
    w_a2p = jnp.pad(w_a2, ((0, LANES - GLA_GATE_RANK), (0, 0))).astype(BF16)
    tok = lambda w: pl.BlockSpec((1, tm, w), lambda b_, s: (b_, s, 0))
    sd = lambda w, dt: jax.ShapeDtypeStruct((B, S, w), dt)
    return pl.pallas_call(
        _gla_proj_kernel,
        out_shape=(sd(hk, BF16), sd(hk, BF16), sd(hv, BF16), sd(hk, F32), sd(hv, BF16)),
        grid=(B, S // tm),
        in_specs=[tok(D), _const_spec(w_ext.shape), _const_spec(w_a2p.shape), _const_spec((1, hk))],
        out_specs=(tok(hk), tok(hk), tok(hv), tok(hk), tok(hv)),
        compiler_params=_params(2),
        name="gla_proj",
    )(x, w_ext, w_a2p, _row(b_a))


def _gla_tables(C):
    levels = [1 << e for e in range(int(math.log2(C)))][::-1]
    t = np.arange(C)
    u = np.arange(C)[None, :]
    blocks = []
    for s in levels:
        ref = (t // (2 * s)) * (2 * s) + s - 1
        upper = ((t // s) % 2) == 1
        q_role = (u > ref[:, None]) & (u <= t[:, None])
        k_role = (u > t[:, None]) & (u <= ref[:, None])
        blocks.append(np.where(upper[:, None], q_role, k_role))
    blocks.append(u <= t[:, None])
    blocks.append(u > t[:, None])
    sums = np.concatenate(blocks, axis=0).astype(np.float32)
    x = t[:, None] ^ t[None, :]
    lvl = np.full((C, C), -1, np.int32)
    for n, s in enumerate(levels):
        lvl = np.where((t[:, None] > t[None, :]) & (x >= s) & (x < 2 * s), n, lvl)
    lvl = np.where(t[:, None] == t[None, :], len(levels), lvl)
    return len(levels), sums, lvl.astype(np.int32)


def _gla_core_kernel(q_ref, k_ref, v_ref, la_ref, gate_ref, norm_ref, sums_ref, lvl_ref, o_ref, state_ref,
                     *, n_levels):
    C = GLA_CHUNK
    DK, DV = GLA_DK, GLA_DV

    @pl.when(pl.program_id(1) == 0)
    def _():
        state_ref[...] = jnp.zeros_like(state_ref)

    lvl = lvl_ref[...]
    for h in range(GLA_HEADS):
        q = q_ref[0, :, h * DK:(h + 1) * DK].astype(F32)
        k = k_ref[0, :, h * DK:(h + 1) * DK].astype(F32)
        qb = q_ref[0, :, h * DK:(h + 1) * DK]
        kb = k_ref[0, :, h * DK:(h + 1) * DK]
        v = v_ref[0, :, h * DV:(h + 1) * DV]
        la = la_ref[0, :, h * DK:(h + 1) * DK]
        hi = la.astype(BF16)
        lo = (la - hi.astype(F32)).astype(BF16)
        e2 = _dot(sums_ref[...], jnp.concatenate([hi, lo], axis=1))
        e = e2[:, :DK] + e2[:, DK:]

        a = jnp.where(lvl == n_levels, _dot_nt(qb, kb), 0.0)
        for n in range(n_levels):
            w = jnp.exp(e[n * C:(n + 1) * C, :])
            p = _dot_nt((q * w).astype(BF16), (k * w).astype(BF16))
            a = jnp.where(lvl == n, p, a)
        e_q = e[n_levels * C:(n_levels + 1) * C, :]
        e_k = e[(n_levels + 1) * C:(n_levels + 2) * C, :]
        st = state_ref[h]
        o = _dot_nt((q * jnp.exp(e_q)).astype(BF16), st.astype(BF16)) + _dot(a.astype(BF16), v)
        k_dec = (k * jnp.exp(e_k)).astype(BF16)
        state_ref[h] = jnp.exp(e_q[C - 1:C, :]) * st + _dot_tn(v, k_dec)
        o = _rmsnorm(o, norm_ref[...])
        o_ref[0, :, h * DV:(h + 1) * DV] = (o * gate_ref[0, :, h * DV:(h + 1) * DV].astype(F32)).astype(BF16)


def _gla_core(q, k, v, la, gate, out_norm):
    B, S, hk = q.shape
    hv = v.shape[-1]
    C = GLA_CHUNK
    n_levels, sums, lvl = _gla_tables(C)
    tok = lambda w: pl.BlockSpec((1, C, w), lambda b_, s: (b_, s, 0))
    return pl.pallas_call(
        functools.partial(_gla_core_kernel, n_levels=n_levels),
        out_shape=jax.ShapeDtypeStruct((B, S, hv), BF16),
        grid=(B, S // C),
        in_specs=[tok(hk), tok(hk), tok(hv), tok(hk), tok(hv), _const_spec((1, GLA_DV)),
                  _const_spec(sums.shape), _const_spec(lvl.shape)],
        out_specs=tok(hv),
        scratch_shapes=[pltpu.VMEM((GLA_HEADS, GLA_DV, GLA_DK), F32)],
        compiler_params=_params(2),
        name="gla_core",
    )(q, k, v, la, gate, _row(out_norm), jnp.asarray(sums, BF16), jnp.asarray(lvl))


def _conformer_kernel(x_ref, w_in_ref, b_in_ref, dw_ref, dwb_ref, lng_ref, lnb_ref, w_o_ref, b_o_ref,
                      g1_ref, b1_ref, o_ref, hist_ref, y_ref, *, tm):
    D = D_MODEL
    HALO = CONV_HALO

    @pl.when(pl.program_id(1) == 0)
    def _():
        hist_ref[0:HALO, :] = jnp.zeros((HALO, D), F32)

    x = x_ref[0]
    h = _dot(x.astype(BF16), w_in_ref[...]) + b_in_ref[...]
    hist_ref[HALO:HALO + tm, :] = h[:, :D] * _sigmoid(h[:, D:])

    n_m = (CONV_WIDTH + SUBLANES - 1) // SUBLANES
    for c0 in range(0, D, LANES):
        y = None
        for r in range(SUBLANES):
            z = None
            for m in range(n_m):
                d = SUBLANES * m + r
                if d >= CONV_WIDTH:
                    continue
                start = HALO - SUBLANES - SUBLANES * m
                tap = CONV_WIDTH - 1 - d
                term = dw_ref[tap:tap + 1, c0:c0 + LANES] * hist_ref[start:start + tm + SUBLANES, c0:c0 + LANES]
                z = term if z is None else z + term
            if r:
                z = pltpu.roll(z, r, 0)
            y = z if y is None else y + z
        y_ref[:, c0:c0 + LANES] = y[SUBLANES:, :] + dwb_ref[:, c0:c0 + LANES]
    hist_ref[0:HALO, :] = hist_ref[tm:tm + HALO, :]

    t = _layernorm(y_ref[...], lng_ref[...], lnb_ref[...])
    t = (t * _sigmoid(t)).astype(BF16)
    out = _dot(t, w_o_ref[...]) + b_o_ref[...]
    o_ref[0] = _layernorm(DEEPNORM_ALPHA * x + out, g1_ref[...], b1_ref[...])


def _conformer(x, w_in, b_in, dw, dw_b, ln_g, ln_b, w_o, b_o, g1, b1, tm):
    B, S, D = x.shape
    return pl.pallas_call(
        functools.partial(_conformer_kernel, tm=tm),
        out_shape=jax.ShapeDtypeStruct((B, S, D), F32),
        grid=(B, S // tm),
        in_specs=[pl.BlockSpec((1, tm, D), lambda b_, s: (b_, s, 0)),
                  _const_spec((D, 2 * D)), _const_spec((1, 2 * D)), _const_spec((CONV_WIDTH, D)),
                  _const_spec((1, D)), _const_spec((1, D)), _const_spec((1, D)), _const_spec((D, D)),
                  _const_spec((1, D)), _const_spec((1, D)), _const_spec((1, D))],
        out_specs=pl.BlockSpec((1, tm, D), lambda b_, s: (b_, s, 0)),
        scratch_shapes=[pltpu.VMEM((CONV_HALO + tm, D), F32), pltpu.VMEM((tm, D), F32)],
        compiler_params=_params(2),
        name="conformer_ln",
    )(x, w_in.astype(BF16), _row(b_in), dw.astype(F32), _row(dw_b), _row(ln_g), _row(ln_b),
      w_o.astype(BF16), _row(b_o), _row(g1), _row(b1))


def _tile(S, want):
    t = min(want, S)
    assert S % t == 0
    return t


def _mla_layer(x, rope, w_in, q_norm, kv_norm, w_uq, w_ukv, w_o, g, b):
    S = x.shape[1]
    q, k, v = _mla_proj(x, rope[0], rope[1], w_in, q_norm, kv_norm, w_uq, w_ukv, _tile(S, 512))
    o = _flash(q, k, v, _tile(S, 512))
    return _out_proj(o, x, w_o, jnp.zeros((D_MODEL,), F32), g, b, _tile(S, 512))


def _gla_layer(x, w_in, w_a2, b_a, out_norm, w_o, g, b):
    S = x.shape[1]
    q, k, v, la, gate = _gla_proj(x, w_in, w_a2, b_a, _tile(S, 512))
    o = _gla_core(q, k, v, la, gate, out_norm)
    return _out_proj(o, x, w_o, jnp.zeros((D_MODEL,), F32), g, b, _tile(S, 512))


def kernel(x, positions, l0_mla_w_in, l0_mla_q_norm, l0_mla_kv_norm, l0_mla_w_uq, l0_mla_w_ukv, l0_mla_w_o, l0_ln1_g, l0_ln1_b, l0_ffn_w_in, l0_ffn_conv, l0_ffn_conv_b, l0_ffn_w_out, l0_ln2_g, l0_ln2_b, l1_gla_w_in, l1_gla_w_a2, l1_gla_b_a, l1_gla_out_norm, l1_gla_w_o, l1_ln1_g, l1_ln1_b, l1_ffn_w_in, l1_ffn_conv, l1_ffn_conv_b, l1_ffn_w_out, l1_ln2_g, l1_ln2_b, l2_conv_w_in, l2_conv_b_in, l2_conv_dw, l2_conv_dw_b, l2_conv_ln_g, l2_conv_ln_b, l2_conv_w_o, l2_conv_b_o, l2_ln1_g, l2_ln1_b, l2_ffn_w_in, l2_ffn_conv, l2_ffn_conv_b, l2_ffn_w_out, l2_ln2_g, l2_ln2_b, l3_mla_w_in, l3_mla_q_norm, l3_mla_kv_norm, l3_mla_w_uq, l3_mla_w_ukv, l3_mla_w_o, l3_ln1_g, l3_ln1_b, l3_ffn_w_in, l3_ffn_conv, l3_ffn_conv_b, l3_ffn_w_out, l3_ln2_g, l3_ln2_b):
    S = x.shape[1]
    assert S % GLA_CHUNK == 0 or S < GLA_CHUNK
    rope = _rope_table(positions, _tile(S, 512))
    ffn_tm = _tile(S, 512)
    ffn_fc = FFN_DIM // 2

    x = _mla_layer(x, rope, l0_mla_w_in, l0_mla_q_norm, l0_mla_kv_norm, l0_mla_w_uq, l0_mla_w_ukv, l0_mla_w_o,
                   l0_ln1_g, l0_ln1_b)
    x = _ffn(x, l0_ffn_w_in, l0_ffn_conv, l0_ffn_conv_b, l0_ffn_w_out, l0_ln2_g, l0_ln2_b, ffn_tm, ffn_fc)

    x = _gla_layer(x, l1_gla_w_in, l1_gla_w_a2, l1_gla_b_a, l1_gla_out_norm, l1_gla_w_o, l1_ln1_g, l1_ln1_b)
    x = _ffn(x, l1_ffn_w_in, l1_ffn_conv, l1_ffn_conv_b, l1_ffn_w_out, l1_ln2_g, l1_ln2_b, ffn_tm, ffn_fc)

    x = _conformer(x, l2_conv_w_in, l2_conv_b_in, l2_conv_dw, l2_conv_dw_b, l2_conv_ln_g, l2_conv_ln_b,
                   l2_conv_w_o, l2_conv_b_o, l2_ln1_g, l2_ln1_b, _tile(S, 256))
    x = _ffn(x, l2_ffn_w_in, l2_ffn_conv, l2_ffn_conv_b, l2_ffn_w_out, l2_ln2_g, l2_ln2_b, ffn_tm, ffn_fc)

    x = _mla_layer(x, rope, l3_mla_w_in, l3_mla_q_norm, l3_mla_kv_norm, l3_mla_w_uq, l3_mla_w_ukv, l3_mla_w_o,
                   l3_ln1_g, l3_ln1_b)
    x = _ffn(x, l3_ffn_w_in, l3_ffn_conv, l3_ffn_conv_b, l3_ffn_w_out, l3_ln2_g, l3_ln2_b, ffn_tm, ffn_fc)
    return x
```

```python
import functools
import math

import numpy as np
import jax
import jax.numpy as jnp
from jax import lax
from jax.experimental import pallas as pl
from jax.experimental.pallas import tpu as pltpu

F32 = jnp.float32
BF16 = jnp.bfloat16

D_MODEL = 1024
DEPTH = 4
MLA_HEADS = 8
MLA_NOPE = 128
MLA_ROPE = 64
MLA_V = 128
MLA_Q_RANK = 256
MLA_KV_RANK = 128
MLA_QK_PAD = 256
MLA_KV_BLOCK = 512
MLA_Q_BLOCK = 1024
ROPE_THETA = 10000.0
GLA_HEADS = 4
GLA_DK = 128
GLA_DV = 256
GLA_GATE_RANK = 16
GLA_GATE_NORM = 16.0
GLA_CHUNK = 256
CONV_WIDTH = 31
CONV_HALO = 32
FFN_DIM = 2816
FFN_CONV_WIDTH = 3
NORM_EPS = 1e-5
DEEPNORM_ALPHA = (2.0 * DEPTH) ** 0.25
SUBLANES = 8
LANES = 128
VMEM_LIMIT = 56 * 1024 * 1024


def _dot(a, b):
    return jnp.dot(a, b, preferred_element_type=F32)


def _dot_nt(a, b):
    return lax.dot_general(a, b, (((1,), (1,)), ((), ())), preferred_element_type=F32)


def _dot_tn(a, b):
    return lax.dot_general(a, b, (((0,), (0,)), ((), ())), preferred_element_type=F32)


def _layernorm(y, g, b):
    mu = jnp.mean(y, axis=-1, keepdims=True)
    d = y - mu
    var = jnp.mean(d * d, axis=-1, keepdims=True)
    return d * lax.rsqrt(var + NORM_EPS) * g + b


def _rmsnorm(y, g):
    return y * lax.rsqrt(jnp.mean(y * y, axis=-1, keepdims=True) + NORM_EPS) * g


def _sigmoid(x):
    return 1.0 / (1.0 + jnp.exp(-x))


def _const_spec(shape):
    nd = len(shape)
    return pl.BlockSpec(shape, lambda *_: (0,) * nd, pipeline_mode=pl.Buffered(1))


def _params(n_axes):
    return pltpu.CompilerParams(dimension_semantics=("arbitrary",) * n_axes, vmem_limit_bytes=VMEM_LIMIT)


def _row(v):
    return v.reshape(1, -1).astype(F32)


def _rope_table_kernel(pos_ref, freq_ref, cos_ref, sin_ref):
    ang = pos_ref[0] * freq_ref[...]
    cos_ref[0] = jnp.cos(ang)
    sin_ref[0] = jnp.sin(ang)


def _rope_table(positions, tm):
    B, S = positions.shape
    half = MLA_ROPE // 2
    inv_freq = ROPE_THETA ** (-jnp.arange(half, dtype=F32) / half)
    freq = jnp.tile(inv_freq, LANES // half).reshape(1, LANES)
    posf = positions.astype(F32).reshape(B, S, 1)
    out = jax.ShapeDtypeStruct((B, S, LANES), F32)
    return pl.pallas_call(
        _rope_table_kernel,
        out_shape=(out, out),
        grid=(B, S // tm),
        in_specs=[pl.BlockSpec((1, tm, 1), lambda b, s: (b, s, 0)), _const_spec((1, LANES))],
        out_specs=(pl.BlockSpec((1, tm, LANES), lambda b, s: (b, s, 0)),
                   pl.BlockSpec((1, tm, LANES), lambda b, s: (b, s, 0))),
        compiler_params=_params(2),
        name="rope_table",
    )(posf, freq)


def _mla_proj_kernel(x_ref, cos_ref, sin_ref, w_in_ref, qn_ref, kvn_ref, w_uq_ref, w_ukt_ref, w_uv_ref,
                     q_ref, kt_ref, v_ref):
    H = MLA_HEADS
    tm = x_ref.shape[1]
    x = x_ref[0].astype(BF16)
    c = _dot(x, w_in_ref[...])
    cos = cos_ref[0]
    sin = sin_ref[0]
    c_q = _rmsnorm(c[:, :MLA_Q_RANK], qn_ref[...]).astype(BF16)
    c_kv = _rmsnorm(c[:, MLA_Q_RANK:MLA_Q_RANK + MLA_KV_RANK], kvn_ref[...])
    o = MLA_Q_RANK + MLA_KV_RANK
    k_rope_t = (c[:, o:o + LANES] * cos + c[:, o + LANES:o + 2 * LANES] * sin).T.astype(BF16)
    scale = (MLA_NOPE + MLA_ROPE) ** -0.5 * math.log2(math.e)
    q_all = _dot(c_q, w_uq_ref[...]) * scale
    k_nope_t = _dot(w_ukt_ref[...], c_kv.T.astype(BF16))
    v = _dot(c_kv.astype(BF16), w_uv_ref[...])
    lane = lax.broadcasted_iota(jnp.int32, (tm, LANES), 1)
    ones_col = jnp.where(lane == 0, 1.0, 0.0).astype(BF16)
    for h in range(H):
        q_ref[0, h, :, 0:LANES] = q_all[:, h * LANES:(h + 1) * LANES].astype(BF16)
        qr = (q_all[:, (H + h) * LANES:(H + h + 1) * LANES] * cos
              + q_all[:, (2 * H + h) * LANES:(2 * H + h + 1) * LANES] * sin)
        q_ref[0, h, :, LANES:2 * LANES] = qr.astype(BF16)
        kt_ref[0, h, 0, 0:LANES, :] = k_nope_t[h * LANES:(h + 1) * LANES, :].astype(BF16)
        kt_ref[0, h, 0, LANES:2 * LANES, :] = k_rope_t
        v_ref[0, h, :, 0:LANES] = v[:, h * LANES:(h + 1) * LANES].astype(BF16)
        v_ref[0, h, :, LANES:2 * LANES] = ones_col


def _rot_half_cols(w):
    half = w.shape[1] // 2
    return jnp.concatenate([-w[:, half:], w[:, :half]], axis=1)


def _mla_proj(x, cos, sin, w_in, q_norm, kv_norm, w_uq, w_ukv, tm):
    B, S, D = x.shape
    H = MLA_HEADS
    o = MLA_Q_RANK + MLA_KV_RANK
    kr = w_in[:, o:]
    krot = _rot_half_cols(kr)
    w_in_ext = jnp.concatenate([w_in[:, :o], kr, kr, krot, krot], axis=1).astype(BF16)
    w_uq3 = w_uq.reshape(MLA_Q_RANK, H, MLA_NOPE + MLA_ROPE)
    nope = w_uq3[:, :, :MLA_NOPE].reshape(MLA_Q_RANK, H * MLA_NOPE)
    rope = w_uq3[:, :, MLA_NOPE:]
    rot = jnp.concatenate([-rope[:, :, MLA_ROPE // 2:], rope[:, :, :MLA_ROPE // 2]], axis=2)
    zpad = jnp.zeros((MLA_Q_RANK, H, LANES - MLA_ROPE), w_uq.dtype)
    rope_p = jnp.concatenate([rope, zpad], axis=2).reshape(MLA_Q_RANK, H * LANES)
    rot_p = jnp.concatenate([rot, zpad], axis=2).reshape(MLA_Q_RANK, H * LANES)
    w_uq_ext = jnp.concatenate([nope, rope_p, rot_p], axis=1).astype(BF16)
    w_ukv3 = w_ukv.reshape(MLA_KV_RANK, H, MLA_NOPE + MLA_V)
    w_ukt = w_ukv3[:, :, :MLA_NOPE].reshape(MLA_KV_RANK, H * MLA_NOPE).T.astype(BF16)
    w_uv = w_ukv3[:, :, MLA_NOPE:].reshape(MLA_KV_RANK, H * MLA_V).astype(BF16)
    qv = jax.ShapeDtypeStruct((B, H, S, MLA_QK_PAD), BF16)
    kt = jax.ShapeDtypeStruct((B, H, S // tm, MLA_QK_PAD, tm), BF16)
    tok = lambda w: pl.BlockSpec((1, tm, w), lambda b, s: (b, s, 0))
    hd = pl.BlockSpec((1, H, tm, MLA_QK_PAD), lambda b, s: (b, 0, s, 0))
    return pl.pallas_call(
        _mla_proj_kernel,
        out_shape=(qv, kt, qv),
        grid=(B, S // tm),
        in_specs=[tok(D), tok(LANES), tok(LANES), _const_spec(w_in_ext.shape), _const_spec((1, MLA_Q_RANK)),
                  _const_spec((1, MLA_KV_RANK)), _const_spec(w_uq_ext.shape), _const_spec(w_ukt.shape),
                  _const_spec(w_uv.shape)],
        out_specs=(hd, pl.BlockSpec((1, H, 1, MLA_QK_PAD, tm), lambda b, s: (b, 0, s, 0, 0)), hd),
        compiler_params=_params(2),
        name="mla_proj",
    )(x, cos, sin, w_in_ext, _row(q_norm), _row(kv_norm), w_uq_ext, w_ukt, w_uv)


def _flash_kernel(q_ref, kt_ref, v_ref, o_ref, sa_ref, sb_ref, *, bq, bk):
    qi = pl.program_id(2)
    n_sub = bq // bk
    q = q_ref[0, 0]

    def scores(q_rows, j):
        return _dot(q_rows, kt_ref[0, 0, j])

    def update(s, j, m, acc, masked):
        start = pl.multiple_of(j * bk, bk)
        v = v_ref[0, 0, pl.ds(start, bk), :]
        if masked:
            row = lax.broadcasted_iota(jnp.int32, s.shape, 0)
            col = lax.broadcasted_iota(jnp.int32, s.shape, 1)
            s = jnp.where(col <= row, s, -jnp.inf)
        m_new = jnp.maximum(m, jnp.max(s, axis=-1, keepdims=True))
        p = jnp.exp2(s - m_new)
        acc = jnp.exp2(m - m_new) * acc + _dot(p.astype(BF16), v)
        return m_new, acc

    def body(t, carry):
        m, acc = carry
        sb_ref[...] = scores(q, 2 * t + 1)
        m, acc = update(sa_ref[...], 2 * t, m, acc, False)
        sa_ref[...] = scores(q, 2 * t + 2)
        m, acc = update(sb_ref[...], 2 * t + 1, m, acc, False)
        return m, acc

    n_off = n_sub * qi
    m = jnp.full((bq, 1), -jnp.inf, F32)
    acc = jnp.zeros((bq, 2 * MLA_V), F32)
    sa_ref[...] = scores(q, 0)
    m, acc = lax.fori_loop(0, n_off // 2, body, (m, acc))
    s = sa_ref[...]
    for t in range(n_sub):
        r0 = t * bk
        s_next = scores(q[r0 + bk:], n_off + t + 1) if t + 1 < n_sub else None
        m_t, acc_t = update(s, n_off + t, m[r0:], acc[r0:], True)
        m = m_t if t == 0 else jnp.concatenate([m[:r0], m_t], axis=0)
        acc = acc_t if t == 0 else jnp.concatenate([acc[:r0], acc_t], axis=0)
        s = s_next
    o_ref[0] = (acc[:, :MLA_V] / acc[:, MLA_V:MLA_V + 1]).astype(o_ref.dtype)


def _flash(q, kt, v, bq):
    B, H, S, _ = q.shape
    n_kv, _, bk = kt.shape[2:]
    assert (bq // bk) % 2 == 0 or S == bq
    return pl.pallas_call(
        functools.partial(_flash_kernel, bq=bq, bk=bk),
        out_shape=jax.ShapeDtypeStruct((B, S, H * MLA_V), BF16),
        grid=(B, H, S // bq),
        in_specs=[pl.BlockSpec((1, 1, bq, MLA_QK_PAD), lambda b, h, i: (b, h, i, 0)),
                  pl.BlockSpec((1, 1, n_kv, MLA_QK_PAD, bk), lambda b, h, i: (b, h, 0, 0, 0)),
                  pl.BlockSpec((1, 1, S, MLA_QK_PAD), lambda b, h, i: (b, h, 0, 0))],
        out_specs=pl.BlockSpec((1, bq, MLA_V), lambda b, h, i: (b, i, h)),
        scratch_shapes=[pltpu.VMEM((bq, bk), F32), pltpu.VMEM((bq, bk), F32)],
        compiler_params=_params(3),
        name="mla_flash",
    )(q, kt, v)


def _out_proj_kernel(a_ref, x_ref, w_ref, bias_ref, g_ref, b_ref, o_ref):
    h = _dot(a_ref[0], w_ref[...]) + bias_ref[...]
    o_ref[0] = _layernorm(DEEPNORM_ALPHA * x_ref[0] + h, g_ref[...], b_ref[...])


def _out_proj(a, x, w, bias, g, b, tm):
    B, S, D = x.shape
    K = a.shape[-1]
    return pl.pallas_call(
        _out_proj_kernel,
        out_shape=jax.ShapeDtypeStruct((B, S, D), F32),
        grid=(B, S // tm),
        in_specs=[pl.BlockSpec((1, tm, K), lambda b_, s: (b_, s, 0)),
                  pl.BlockSpec((1, tm, D), lambda b_, s: (b_, s, 0)),
                  _const_spec((K, D)), _const_spec((1, D)), _const_spec((1, D)), _const_spec((1, D))],
        out_specs=pl.BlockSpec((1, tm, D), lambda b_, s: (b_, s, 0)),
        compiler_params=_params(2),
        name="out_proj_ln",
    )(a, x, w.astype(BF16), _row(bias), _row(g), _row(b))


def _ffn_kernel(x_ref, w_in_ref, cw_ref, cb_ref, w_out_ref, g_ref, b_ref, o_ref, halo_ref, *, tm, fc):
    @pl.when(pl.program_id(1) == 0)
    def _():
        halo_ref[...] = jnp.zeros_like(halo_ref)

    x = x_ref[0]
    xb = x.astype(BF16)

    def conv(col0):
        u = _dot(xb, w_in_ref[:, col0:col0 + fc])
        ext = jnp.concatenate([halo_ref[:, col0:col0 + fc], u], axis=0)
        halo_ref[:, col0:col0 + fc] = u[tm - SUBLANES:, :]
        s1 = pltpu.roll(ext, 1, 0)[SUBLANES:, :]
        s2 = pltpu.roll(ext, 2, 0)[SUBLANES:, :]
        w = cw_ref[:, col0:col0 + fc]
        return w[2:3, :] * u + w[1:2, :] * s1 + w[0:1, :] * s2 + cb_ref[:, col0:col0 + fc]

    acc = jnp.zeros((tm, D_MODEL), F32)
    for j in range(FFN_DIM // fc):
        gate = conv(j * fc)
        val = conv(FFN_DIM + j * fc)
        act = (gate * _sigmoid(gate) * val).astype(BF16)
        acc = acc + _dot(act, w_out_ref[j * fc:(j + 1) * fc, :])
    o_ref[0] = _layernorm(DEEPNORM_ALPHA * x + acc, g_ref[...], b_ref[...])


def _ffn(x, w_in, conv_w, conv_b, w_out, g, b, tm, fc):
    B, S, D = x.shape
    return pl.pallas_call(
        functools.partial(_ffn_kernel, tm=tm, fc=fc),
        out_shape=jax.ShapeDtypeStruct((B, S, D), F32),
        grid=(B, S // tm),
        in_specs=[pl.BlockSpec((1, tm, D), lambda b_, s: (b_, s, 0)),
                  _const_spec((D, 2 * FFN_DIM)), _const_spec((FFN_CONV_WIDTH, 2 * FFN_DIM)),
                  _const_spec((1, 2 * FFN_DIM)), _const_spec((FFN_DIM, D)),
                  _const_spec((1, D)), _const_spec((1, D))],
        out_specs=pl.BlockSpec((1, tm, D), lambda b_, s: (b_, s, 0)),
        scratch_shapes=[pltpu.VMEM((SUBLANES, 2 * FFN_DIM), F32)],
        compiler_params=_params(2),
        name="conv_ffn_ln",
    )(x, w_in.astype(BF16), conv_w.astype(F32), _row(conv_b), w_out.astype(BF16), _row(g), _row(b))


def _gla_proj_kernel(x_ref, w_in_ref, w_a2_ref, b_a_ref, q_ref, k_ref, v_ref, la_ref, gate_ref):
    hk = GLA_HEADS * GLA_DK
    hv = GLA_HEADS * GLA_DV
    proj = _dot(x_ref[0].astype(BF16), w_in_ref[...])
    q_ref[0] = (proj[:, :hk] * GLA_DK ** -0.5).astype(BF16)
    k_ref[0] = proj[:, hk:2 * hk].astype(BF16)
    v_ref[0] = proj[:, 2 * hk:2 * hk + hv].astype(BF16)
    r = proj[:, 2 * hk + hv:2 * hk + 2 * hv]
    gate_ref[0] = (r * _sigmoid(r)).astype(BF16)
    a_lr = proj[:, 2 * hk + 2 * hv:].astype(BF16)
    z = _dot(a_lr, w_a2_ref[...]) + b_a_ref[...]
    la_ref[0] = (jnp.minimum(z, 0.0) - jnp.log(1.0 + jnp.exp(-jnp.abs(z)))) * (1.0 / GLA_GATE_NORM)


def _gla_proj(x, w_in, w_a2, b_a, tm):
    B, S, D = x.shape
    hk = GLA_HEADS * GLA_DK
    hv = GLA_HEADS * GLA_DV
    a0 = 2 * hk + hv
    a_cols = jnp.pad(w_in[:, a0:a0 + GLA_GATE_RANK], ((0, 0), (0, LANES - GLA_GATE_RANK)))
    w_ext = jnp.concatenate([w_in[:, :a0], w_in[:, a0 + GLA_GATE_RANK:], a_cols], axis=1).astype(BF16)
    w_a2p = jnp.pad(w_a2, ((0, LANES - GLA_GATE_RANK), (0, 0))).astype(BF16)
    tok = lambda w: pl.BlockSpec((1, tm, w), lambda b_, s: (b_, s, 0))
    sd = lambda w, dt: jax.ShapeDtypeStruct((B, S, w), dt)
    return pl.pallas_call(
        _gla_proj_kernel,
        out_shape=(sd(hk, BF16), sd(hk, BF16), sd(hv, BF16), sd(hk, F32), sd(hv, BF16)),
        grid=(B, S // tm),
        in_specs=[tok(D), _const_spec(w_ext.shape), _const_spec(w_a2p.shape), _const_spec((1, hk))],
        out_specs=(tok(hk), tok(hk), tok(hv), tok(hk), tok(hv)),
        compiler_params=_params(2),
        name="gla_proj",
    )(x, w_ext, w_a2p, _row(b_a))


def _gla_tables(C):
    levels = [1 << e for e in range(int(math.log2(C)))][::-1]
    t = np.arange(C)
    u = np.arange(C)[None, :]
    blocks = []
    for s in levels:
        ref = (t // (2 * s)) * (2 * s) + s - 1
        upper = ((t // s) % 2) == 1
        q_role = (u > ref[:, None]) & (u <= t[:, None])
        k_role = (u > t[:, None]) & (u <= ref[:, None])
        blocks.append(np.where(upper[:, None], q_role, k_role))
    blocks.append(u <= t[:, None])
    blocks.append(u > t[:, None])
    sums = np.concatenate(blocks, axis=0).astype(np.float32)
    x = t[:, None] ^ t[None, :]
    lvl = np.full((C, C), -1, np.int32)
    for n, s in enumerate(levels):
        lvl = np.where((t[:, None] > t[None, :]) & (x >= s) & (x < 2 * s), n, lvl)
    lvl = np.where(t[:, None] == t[None, :], len(levels), lvl)
    return len(levels), sums, lvl.astype(np.int32)


def _gla_core_kernel(q_ref, k_ref, v_ref, la_ref, gate_ref, norm_ref, sums_ref, lvl_ref, o_ref, state_ref,
                     *, n_levels):
    C = GLA_CHUNK
    DK, DV = GLA_DK, GLA_DV

    @pl.when(pl.program_id(1) == 0)
    def _():
        state_ref[...] = jnp.zeros_like(state_ref)

    lvl = lvl_ref[...]
    for h in range(GLA_HEADS):
        q = q_ref[0, :, h * DK:(h + 1) * DK].astype(F32)
        k = k_ref[0, :, h * DK:(h + 1) * DK].astype(F32)
        qb = q_ref[0, :, h * DK:(h + 1) * DK]
        kb = k_ref[0, :, h * DK:(h + 1) * DK]
        v = v_ref[0, :, h * DV:(h + 1) * DV]
        la = la_ref[0, :, h * DK:(h + 1) * DK]
        hi = la.astype(BF16)
        lo = (la - hi.astype(F32)).astype(BF16)
        e2 = _dot(sums_ref[...], jnp.concatenate([hi, lo], axis=1))
        e = e2[:, :DK] + e2[:, DK:]

        a = jnp.where(lvl == n_levels, _dot_nt(qb, kb), 0.0)
        for n in range(n_levels):
            w = jnp.exp(e[n * C:(n + 1) * C, :])
            p = _dot_nt((q * w).astype(BF16), (k * w).astype(BF16))
            a = jnp.where(lvl == n, p, a)
        e_q = e[n_levels * C:(n_levels + 1) * C, :]
        e_k = e[(n_levels + 1) * C:(n_levels + 2) * C, :]
        st = state_ref[h]
        o = _dot_nt((q * jnp.exp(e_q)).astype(BF16), st.astype(BF16)) + _dot(a.astype(BF16), v)
        k_dec = (k * jnp.exp(e_k)).astype(BF16)
        state_ref[h] = jnp.exp(e_q[C - 1:C, :]) * st + _dot_tn(v, k_dec)
        o = _rmsnorm(o, norm_ref[...])
        o_ref[0, :, h * DV:(h + 1) * DV] = (o * gate_ref[0, :, h * DV:(h + 1) * DV].astype(F32)).astype(BF16)


def _gla_core(q, k, v, la, gate, out_norm):
    B, S, hk = q.shape
    hv = v.shape[-1]
    C = GLA_CHUNK
    n_levels, sums, lvl = _gla_tables(C)
    tok = lambda w: pl.BlockSpec((1, C, w), lambda b_, s: (b_, s, 0))
    return pl.pallas_call(
        functools.partial(_gla_core_kernel, n_levels=n_levels),
        out_shape=jax.ShapeDtypeStruct((B, S, hv), BF16),
        grid=(B, S // C),
        in_specs=[tok(hk), tok(hk), tok(hv), tok(hk), tok(hv), _const_spec((1, GLA_DV)),
                  _const_spec(sums.shape), _const_spec(lvl.shape)],
        out_specs=tok(hv),
        scratch_shapes=[pltpu.VMEM((GLA_HEADS, GLA_DV, GLA_DK), F32)],
        compiler_params=_params(2),
        name="gla_core",
    )(q, k, v, la, gate, _row(out_norm), jnp.asarray(sums, BF16), jnp.asarray(lvl))


def _conformer_kernel(x_ref, w_in_ref, b_in_ref, dw_ref, dwb_ref, lng_ref, lnb_ref, w_o_ref, b_o_ref,
                      g1_ref, b1_ref, o_ref, hist_ref, y_ref, *, tm):
    D = D_MODEL
    HALO = CONV_HALO

    @pl.when(pl.program_id(1) == 0)
    def _():
        hist_ref[0:HALO, :] = jnp.zeros((HALO, D), F32)

    x = x_ref[0]
    h = _dot(x.astype(BF16), w_in_ref[...]) + b_in_ref[...]
    hist_ref[HALO:HALO + tm, :] = h[:, :D] * _sigmoid(h[:, D:])

    n_m = (CONV_WIDTH + SUBLANES - 1) // SUBLANES
    for c0 in range(0, D, LANES):
        y = None
        for r in range(SUBLANES):
            z = None
            for m in range(n_m):
                d = SUBLANES * m + r
                if d >= CONV_WIDTH:
                    continue
                start = HALO - SUBLANES - SUBLANES * m
                tap = CONV_WIDTH - 1 - d
                term = dw_ref[tap:tap + 1, c0:c0 + LANES] * hist_ref[start:start + tm + SUBLANES, c0:c0 + LANES]
                z = term if z is None else z + term
            if r:
                z = pltpu.roll(z, r, 0)
            y = z if y is None else y + z
        y_ref[:, c0:c0 + LANES] = y[SUBLANES:, :] + dwb_ref[:, c0:c0 + LANES]
    hist_ref[0:HALO, :] = hist_ref[tm:tm + HALO, :]

    t = _layernorm(y_ref[...], lng_ref[...], lnb_ref[...])
    t = (t * _sigmoid(t)).astype(BF16)
    out = _dot(t, w_o_ref[...]) + b_o_ref[...]
    o_ref[0] = _layernorm(DEEPNORM_ALPHA * x + out, g1_ref[...], b1_ref[...])


def _conformer(x, w_in, b_in, dw, dw_b, ln_g, ln_b, w_o, b_o, g1, b1, tm):
    B, S, D = x.shape
    return pl.pallas_call(
        functools.partial(_conformer_kernel, tm=tm),
        out_shape=jax.ShapeDtypeStruct((B, S, D), F32),
        grid=(B, S // tm),
        in_specs=[pl.BlockSpec((1, tm, D), lambda b_, s: (b_, s, 0)),
                  _const_spec((D, 2 * D)), _const_spec((1, 2 * D)), _const_spec((CONV_WIDTH, D)),
                  _const_spec((1, D)), _const_spec((1, D)), _const_spec((1, D)), _const_spec((D, D)),
                  _const_spec((1, D)), _const_spec((1, D)), _const_spec((1, D))],
        out_specs=pl.BlockSpec((1, tm, D), lambda b_, s: (b_, s, 0)),
        scratch_shapes=[pltpu.VMEM((CONV_HALO + tm, D), F32), pltpu.VMEM((tm, D), F32)],
        compiler_params=_params(2),
        name="conformer_ln",
    )(x, w_in.astype(BF16), _row(b_in), dw.astype(F32), _row(dw_b), _row(ln_g), _row(ln_b),
      w_o.astype(BF16), _row(b_o), _row(g1), _row(b1))


def _tile(S, want):
    t = min(want, S)
    assert S % t == 0
    return t


def _mla_layer(x, rope, w_in, q_norm, kv_norm, w_uq, w_ukv, w_o, g, b):
    S = x.shape[1]
    q, kt, v = _mla_proj(x, rope[0], rope[1], w_in, q_norm, kv_norm, w_uq, w_ukv, _tile(S, MLA_KV_BLOCK))
    o = _flash(q, kt, v, _tile(S, MLA_Q_BLOCK))
    return _out_proj(o, x, w_o, jnp.zeros((D_MODEL,), F32), g, b, _tile(S, 512))


def _gla_layer(x, w_in, w_a2, b_a, out_norm, w_o, g, b):
    S = x.shape[1]
    q, k, v, la, gate = _gla_proj(x, w_in, w_a2, b_a, _tile(S, 512))
    o = _gla_core(q, k, v, la, gate, out_norm)
    return _out_proj(o, x, w_o, jnp.zeros((D_MODEL,), F32), g, b, _tile(S, 512))


def kernel(x, positions, l0_mla_w_in, l0_mla_q_norm, l0_mla_kv_norm, l0_mla_w_uq, l0_mla_w_ukv, l0_mla_w_o, l0_ln1_g, l0_ln1_b, l0_ffn_w_in, l0_ffn_conv, l0_ffn_conv_b, l0_ffn_w_out, l0_ln2_g, l0_ln2_b, l1_gla_w_in, l1_gla_w_a2, l1_gla_b_a, l1_gla_out_norm, l1_gla_w_o, l1_ln1_g, l1_ln1_b, l1_ffn_w_in, l1_ffn_conv, l1_ffn_conv_b, l1_ffn_w_out, l1_ln2_g, l1_ln2_b, l2_conv_w_in, l2_conv_b_in, l2_conv_dw, l2_conv_dw_b, l2_conv_ln_g, l2_conv_ln_b, l2_conv_w_o, l2_conv_b_o, l2_ln1_g, l2_ln1_b, l2_ffn_w_in, l2_ffn_conv, l2_ffn_conv_b, l2_ffn_w_out, l2_ln2_g, l2_ln2_b, l3_mla_w_in, l3_mla_q_norm, l3_mla_kv_norm, l3_mla_w_uq, l3_mla_w_ukv, l3_mla_w_o, l3_ln1_g, l3_ln1_b, l3_ffn_w_in, l3_ffn_conv, l3_ffn_conv_b, l3_ffn_w_out, l3_ln2_g, l3_ln2_b):
    S = x.shape[1]
    assert S % GLA_CHUNK == 0 or S < GLA_CHUNK
    rope = _rope_table(positions, _tile(S, 512))
    ffn_tm = _tile(S, 512)
    ffn_fc = FFN_DIM // 2

    x = _mla_layer(x, rope, l0_mla_w_in, l0_mla_q_norm, l0_mla_kv_norm, l0_mla_w_uq, l0_mla_w_ukv, l0_mla_w_o,
                   l0_ln1_g, l0_ln1_b)
    x = _ffn(x, l0_ffn_w_in, l0_ffn_conv, l0_ffn_conv_b, l0_ffn_w_out, l0_ln2_g, l0_ln2_b, ffn_tm, ffn_fc)

    x = _gla_layer(x, l1_gla_w_in, l1_gla_w_a2, l1_gla_b_a, l1_gla_out_norm, l1_gla_w_o, l1_ln1_g, l1_ln1_b)
    x = _ffn(x, l1_ffn_w_in, l1_ffn_conv, l1_ffn_conv_b, l1_ffn_w_out, l1_ln2_g, l1_ln2_b, ffn_tm, ffn_fc)

    x = _conformer(x, l2_conv_w_in, l2_conv_b_in, l2_conv_dw, l2_conv_dw_b, l2_conv_ln_g, l2_conv_ln_b,
                   l2_conv_w_o, l2_conv_b_o, l2_ln1_g, l2_ln1_b, _tile(S, 256))
    x = _ffn(x, l2_ffn_w_in, l2_ffn_conv, l2_ffn_conv_b, l2_ffn_w_out, l2_ln2_g, l2_ln2_b, ffn_tm, ffn_fc)

    x = _mla_layer(x, rope, l3_mla_w_in, l3_mla_q_norm, l3_mla_kv_norm, l3_mla_w_uq, l3_mla_w_ukv, l3_mla_w_o,
                   l3_ln1_g, l3_ln1_b)
    x = _ffn(x, l3_ffn_w_in, l3_ffn_conv, l3_ffn_conv_b, l3_ffn_w_out, l3_ln2_g, l3_ln2_b, ffn_tm, ffn_fc)
    return x
```

```python
import functools
import math

import numpy as np
import jax
import jax.numpy as jnp
from jax import lax
from jax.experimental import pallas as pl
from jax.experimental.pallas import tpu as pltpu

F32 = jnp.float32
BF16 = jnp.bfloat16

D_MODEL = 1024
DEPTH = 4
MLA_HEADS = 8
MLA_NOPE = 128
MLA_ROPE = 64
MLA_V = 128
MLA_Q_RANK = 256
MLA_KV_RANK = 128
MLA_QK_PAD = 256
MLA_KV_BLOCK = 512
MLA_Q_BLOCK = 2048
FLASH_UNROLL_PAIRS = 2
ROPE_THETA = 10000.0
GLA_HEADS = 4
GLA_DK = 128
GLA_DV = 256
GLA_GATE_RANK = 16
GLA_GATE_NORM = 16.0
GLA_CHUNK = 256
CONV_WIDTH = 31
CONV_HALO = 32
FFN_DIM = 2816
FFN_CONV_WIDTH = 3
FFN_CHUNKS = ((0, 1536), (1536, 2816))
TOKEN_TILE = 512
CONFORMER_TILE = 256
NORM_EPS = 1e-5
DEEPNORM_ALPHA = (2.0 * DEPTH) ** 0.25
SUBLANES = 8
LANES = 128
VMEM_LIMIT = 56 * 1024 * 1024


def _dot(a, b):
    return jnp.dot(a, b, preferred_element_type=F32)


def _dot_nt(a, b):
    return lax.dot_general(a, b, (((1,), (1,)), ((), ())), preferred_element_type=F32)


def _dot_tn(a, b):
    return lax.dot_general(a, b, (((0,), (0,)), ((), ())), preferred_element_type=F32)


def _layernorm(y, g, b):
    mu = jnp.mean(y, axis=-1, keepdims=True)
    d = y - mu
    var = jnp.mean(d * d, axis=-1, keepdims=True)
    return d * lax.rsqrt(var + NORM_EPS) * g + b


def _rmsnorm(y, g):
    return y * lax.rsqrt(jnp.mean(y * y, axis=-1, keepdims=True) + NORM_EPS) * g


def _sigmoid(x):
    return 1.0 / (1.0 + jnp.exp(-x))


def _const_spec(shape):
    nd = len(shape)
    return pl.BlockSpec(shape, lambda *_: (0,) * nd, pipeline_mode=pl.Buffered(1))


def _params(n_axes):
    return pltpu.CompilerParams(dimension_semantics=("arbitrary",) * n_axes, vmem_limit_bytes=VMEM_LIMIT)


def _row(v):
    return v.reshape(1, -1).astype(F32)


def _rope_table_kernel(pos_ref, freq_ref, cos_ref, sin_ref):
    ang = pos_ref[0] * freq_ref[...]
    cos_ref[0] = jnp.cos(ang)
    sin_ref[0] = jnp.sin(ang)


def _rope_table(positions, tm):
    B, S = positions.shape
    half = MLA_ROPE // 2
    inv_freq = ROPE_THETA ** (-jnp.arange(half, dtype=F32) / half)
    freq = jnp.tile(inv_freq, LANES // half).reshape(1, LANES)
    posf = positions.astype(F32).reshape(B, S, 1)
    out = jax.ShapeDtypeStruct((B, S, LANES), F32)
    return pl.pallas_call(
        _rope_table_kernel,
        out_shape=(out, out),
        grid=(B, S // tm),
        in_specs=[pl.BlockSpec((1, tm, 1), lambda b, s: (b, s, 0)), _const_spec((1, LANES))],
        out_specs=(pl.BlockSpec((1, tm, LANES), lambda b, s: (b, s, 0)),
                   pl.BlockSpec((1, tm, LANES), lambda b, s: (b, s, 0))),
        compiler_params=_params(2),
        name="rope_table",
    )(posf, freq)


def _mla_proj_kernel(x_ref, cos_ref, sin_ref, w_in_ref, qn_ref, kvn_ref, w_uq_ref, w_ukt_ref, w_uv_ref,
                     q_ref, kt_ref, v_ref):
    H = MLA_HEADS
    tm = x_ref.shape[1]
    x = x_ref[0].astype(BF16)
    c = _dot(x, w_in_ref[...])
    cos = cos_ref[0]
    sin = sin_ref[0]
    c_q = _rmsnorm(c[:, :MLA_Q_RANK], qn_ref[...]).astype(BF16)
    c_kv = _rmsnorm(c[:, MLA_Q_RANK:MLA_Q_RANK + MLA_KV_RANK], kvn_ref[...])
    o = MLA_Q_RANK + MLA_KV_RANK
    k_rope_t = (c[:, o:o + LANES] * cos + c[:, o + LANES:o + 2 * LANES] * sin).T.astype(BF16)
    scale = (MLA_NOPE + MLA_ROPE) ** -0.5 * math.log2(math.e)
    q_all = _dot(c_q, w_uq_ref[...]) * scale
    k_nope_t = _dot(w_ukt_ref[...], c_kv.T.astype(BF16))
    v = _dot(c_kv.astype(BF16), w_uv_ref[...])
    lane = lax.broadcasted_iota(jnp.int32, (tm, LANES), 1)
    ones_col = jnp.where(lane == 0, 1.0, 0.0).astype(BF16)
    for h in range(H):
        q_ref[0, h, :, 0:LANES] = q_all[:, h * LANES:(h + 1) * LANES].astype(BF16)
        qr = (q_all[:, (H + h) * LANES:(H + h + 1) * LANES] * cos
              + q_all[:, (2 * H + h) * LANES:(2 * H + h + 1) * LANES] * sin)
        q_ref[0, h, :, LANES:2 * LANES] = qr.astype(BF16)
        kt_ref[0, h, 0, 0:LANES, :] = k_nope_t[h * LANES:(h + 1) * LANES, :].astype(BF16)
        kt_ref[0, h, 0, LANES:2 * LANES, :] = k_rope_t
        v_ref[0, h, :, 0:LANES] = v[:, h * LANES:(h + 1) * LANES].astype(BF16)
        v_ref[0, h, :, LANES:2 * LANES] = ones_col


def _rot_half_cols(w):
    half = w.shape[1] // 2
    return jnp.concatenate([-w[:, half:], w[:, :half]], axis=1)


def _mla_proj(x, cos, sin, w_in, q_norm, kv_norm, w_uq, w_ukv, tm):
    B, S, D = x.shape
    H = MLA_HEADS
    o = MLA_Q_RANK + MLA_KV_RANK
    kr = w_in[:, o:]
    krot = _rot_half_cols(kr)
    w_in_ext = jnp.concatenate([w_in[:, :o], kr, kr, krot, krot], axis=1).astype(BF16)
    w_uq3 = w_uq.reshape(MLA_Q_RANK, H, MLA_NOPE + MLA_ROPE)
    nope = w_uq3[:, :, :MLA_NOPE].reshape(MLA_Q_RANK, H * MLA_NOPE)
    rope = w_uq3[:, :, MLA_NOPE:]
    rot = jnp.concatenate([-rope[:, :, MLA_ROPE // 2:], rope[:, :, :MLA_ROPE // 2]], axis=2)
    zpad = jnp.zeros((MLA_Q_RANK, H, LANES - MLA_ROPE), w_uq.dtype)
    rope_p = jnp.concatenate([rope, zpad], axis=2).reshape(MLA_Q_RANK, H * LANES)
    rot_p = jnp.concatenate([rot, zpad], axis=2).reshape(MLA_Q_RANK, H * LANES)
    w_uq_ext = jnp.concatenate([nope, rope_p, rot_p], axis=1).astype(BF16)
    w_ukv3 = w_ukv.reshape(MLA_KV_RANK, H, MLA_NOPE + MLA_V)
    w_ukt = w_ukv3[:, :, :MLA_NOPE].reshape(MLA_KV_RANK, H * MLA_NOPE).T.astype(BF16)
    w_uv = w_ukv3[:, :, MLA_NOPE:].reshape(MLA_KV_RANK, H * MLA_V).astype(BF16)
    qv = jax.ShapeDtypeStruct((B, H, S, MLA_QK_PAD), BF16)
    kt = jax.ShapeDtypeStruct((B, H, S // tm, MLA_QK_PAD, tm), BF16)
    tok = lambda w: pl.BlockSpec((1, tm, w), lambda b, s: (b, s, 0))
    hd = pl.BlockSpec((1, H, tm, MLA_QK_PAD), lambda b, s: (b, 0, s, 0))
    return pl.pallas_call(
        _mla_proj_kernel,
        out_shape=(qv, kt, qv),
        grid=(B, S // tm),
        in_specs=[tok(D), tok(LANES), tok(LANES), _const_spec(w_in_ext.shape), _const_spec((1, MLA_Q_RANK)),
                  _const_spec((1, MLA_KV_RANK)), _const_spec(w_uq_ext.shape), _const_spec(w_ukt.shape),
                  _const_spec(w_uv.shape)],
        out_specs=(hd, pl.BlockSpec((1, H, 1, MLA_QK_PAD, tm), lambda b, s: (b, 0, s, 0, 0)), hd),
        compiler_params=_params(2),
        name="mla_proj",
    )(x, cos, sin, w_in_ext, _row(q_norm), _row(kv_norm), w_uq_ext, w_ukt, w_uv)


def _flash_kernel(q_ref, kt_ref, v_ref, o_ref, sa_ref, sb_ref, *, bq, bk):
    qi = pl.program_id(2)
    n_sub = bq // bk
    q = q_ref[0, 0]

    def scores(q_rows, j):
        return _dot(q_rows, kt_ref[0, 0, j])

    def update(s, j, m, acc, masked):
        start = pl.multiple_of(j * bk, bk)
        v = v_ref[0, 0, pl.ds(start, bk), :]
        if masked:
            row = lax.broadcasted_iota(jnp.int32, s.shape, 0)
            col = lax.broadcasted_iota(jnp.int32, s.shape, 1)
            s = jnp.where(col <= row, s, -jnp.inf)
        m_new = jnp.maximum(m, jnp.max(s, axis=-1, keepdims=True))
        p = jnp.exp2(s - m_new)
        acc = jnp.exp2(m - m_new) * acc + _dot(p.astype(BF16), v)
        return m_new, acc

    def pairs(n_pairs, base):
        def body(t, carry):
            m, acc = carry
            j = base + 2 * n_pairs * t
            for u in range(n_pairs):
                sb_ref[...] = scores(q, j + 2 * u + 1)
                m, acc = update(sa_ref[...], j + 2 * u, m, acc, False)
                sa_ref[...] = scores(q, j + 2 * u + 2)
                m, acc = update(sb_ref[...], j + 2 * u + 1, m, acc, False)
            return m, acc
        return body

    n_off = n_sub * qi
    m = jnp.full((bq, 1), -jnp.inf, F32)
    acc = jnp.zeros((bq, 2 * MLA_V), F32)
    sa_ref[...] = scores(q, 0)
    n_long = n_off // (2 * FLASH_UNROLL_PAIRS)
    m, acc = lax.fori_loop(0, n_long, pairs(FLASH_UNROLL_PAIRS, 0), (m, acc))
    if n_sub % (2 * FLASH_UNROLL_PAIRS):
        done = 2 * FLASH_UNROLL_PAIRS * n_long
        m, acc = lax.fori_loop(0, (n_off - done) // 2, pairs(1, done), (m, acc))
    s = sa_ref[...]
    for t in range(n_sub):
        r0 = t * bk
        s_next = scores(q[r0 + bk:], n_off + t + 1) if t + 1 < n_sub else None
        m_t, acc_t = update(s, n_off + t, m[r0:], acc[r0:], True)
        m = m_t if t == 0 else jnp.concatenate([m[:r0], m_t], axis=0)
        acc = acc_t if t == 0 else jnp.concatenate([acc[:r0], acc_t], axis=0)
        s = s_next
    o_ref[0] = (acc[:, :MLA_V] / acc[:, MLA_V:MLA_V + 1]).astype(o_ref.dtype)


def _flash(q, kt, v, bq):
    B, H, S, _ = q.shape
    n_kv, _, bk = kt.shape[2:]
    assert (bq // bk) % 2 == 0 or S == bq
    return pl.pallas_call(
        functools.partial(_flash_kernel, bq=bq, bk=bk),
        out_shape=jax.ShapeDtypeStruct((B, S, H * MLA_V), BF16),
        grid=(B, H, S // bq),
        in_specs=[pl.BlockSpec((1, 1, bq, MLA_QK_PAD), lambda b, h, i: (b, h, i, 0)),
                  pl.BlockSpec((1, 1, n_kv, MLA_QK_PAD, bk), lambda b, h, i: (b, h, 0, 0, 0)),
                  pl.BlockSpec((1, 1, S, MLA_QK_PAD), lambda b, h, i: (b, h, 0, 0))],
        out_specs=pl.BlockSpec((1, bq, MLA_V), lambda b, h, i: (b, i, h)),
        scratch_shapes=[pltpu.VMEM((bq, bk), F32), pltpu.VMEM((bq, bk), F32)],
        compiler_params=_params(3),
        name="mla_flash",
    )(q, kt, v)


def _ffn_body(x, w_in_ref, cw_ref, cb_ref, w_out_ref, g_ref, b_ref, o_ref, halo_ref):
    tm = x.shape[0]

    @pl.when(pl.program_id(1) == 0)
    def _():
        halo_ref[...] = jnp.zeros_like(halo_ref)

    xb = x.astype(BF16)

    def conv(c0, c1):
        u = _dot(xb, w_in_ref[:, c0:c1])
        ext = jnp.concatenate([halo_ref[:, c0:c1], u], axis=0)
        halo_ref[:, c0:c1] = u[tm - SUBLANES:, :]
        s1 = pltpu.roll(ext, 1, 0)[SUBLANES:, :]
        s2 = pltpu.roll(ext, 2, 0)[SUBLANES:, :]
        w = cw_ref[:, c0:c1]
        return w[2:3, :] * u + w[1:2, :] * s1 + w[0:1, :] * s2 + cb_ref[:, c0:c1]

    acc = jnp.zeros((tm, D_MODEL), F32)
    for c0, c1 in FFN_CHUNKS:
        gate = conv(c0, c1)
        val = conv(FFN_DIM + c0, FFN_DIM + c1)
        act = (gate * _sigmoid(gate) * val).astype(BF16)
        acc = acc + _dot(act, w_out_ref[c0:c1, :])
    o_ref[0] = _layernorm(DEEPNORM_ALPHA * x + acc, g_ref[...], b_ref[...])


def _ffn_kernel(x_ref, *rest):
    _ffn_body(x_ref[0], *rest)


def _mix_ffn_kernel(a_ref, x_ref, w_o_ref, g1_ref, b1_ref, *rest):
    h = _dot(a_ref[0], w_o_ref[...])
    _ffn_body(_layernorm(DEEPNORM_ALPHA * x_ref[0] + h, g1_ref[...], b1_ref[...]), *rest)


def _ffn(x, w_in, conv_w, conv_b, w_out, g, b, tm, mixer=None):
    B, S, D = x.shape
    tok = lambda w: pl.BlockSpec((1, tm, w), lambda b_, s: (b_, s, 0))
    args = [x, w_in.astype(BF16), conv_w.astype(F32), _row(conv_b), w_out.astype(BF16), _row(g), _row(b)]
    specs = [tok(D), _const_spec((D, 2 * FFN_DIM)), _const_spec((FFN_CONV_WIDTH, 2 * FFN_DIM)),
             _const_spec((1, 2 * FFN_DIM)), _const_spec((FFN_DIM, D)), _const_spec((1, D)), _const_spec((1, D))]
    body = _ffn_kernel
    if mixer is not None:
        a, w_o, g1, b1 = mixer
        K = a.shape[-1]
        args = [a, x, w_o.astype(BF16), _row(g1), _row(b1)] + args[1:]
        specs = [tok(K), tok(D), _const_spec((K, D)), _const_spec((1, D)), _const_spec((1, D))] + specs[1:]
        body = _mix_ffn_kernel
    return pl.pallas_call(
        body,
        out_shape=jax.ShapeDtypeStruct((B, S, D), F32),
        grid=(B, S // tm),
        in_specs=specs,
        out_specs=tok(D),
        scratch_shapes=[pltpu.VMEM((SUBLANES, 2 * FFN_DIM), F32)],
        compiler_params=_params(2),
        name="conv_ffn_ln",
    )(*args)


def _gla_proj_kernel(x_ref, w_in_ref, w_a2_ref, b_a_ref, q_ref, k_ref, v_ref, la_ref, gate_ref):
    hk = GLA_HEADS * GLA_DK
    hv = GLA_HEADS * GLA_DV
    proj = _dot(x_ref[0].astype(BF16), w_in_ref[...])
    q_ref[0] = (proj[:, :hk] * GLA_DK ** -0.5).astype(BF16)
    k_ref[0] = proj[:, hk:2 * hk].astype(BF16)
    v_ref[0] = proj[:, 2 * hk:2 * hk + hv].astype(BF16)
    r = proj[:, 2 * hk + hv:2 * hk + 2 * hv]
    gate_ref[0] = (r * _sigmoid(r)).astype(BF16)
    a_lr = proj[:, 2 * hk + 2 * hv:].astype(BF16)
    z = _dot(a_lr, w_a2_ref[...]) + b_a_ref[...]
    la_ref[0] = (jnp.minimum(z, 0.0) - jnp.log(1.0 + jnp.exp(-jnp.abs(z)))) * (1.0 / GLA_GATE_NORM)


def _gla_proj(x, w_in, w_a2, b_a, tm):
    B, S, D = x.shape
    hk = GLA_HEADS * GLA_DK
    hv = GLA_HEADS * GLA_DV
    a0 = 2 * hk + hv
    a_cols = jnp.pad(w_in[:, a0:a0 + GLA_GATE_RANK], ((0, 0), (0, LANES - GLA_GATE_RANK)))
    w_ext = jnp.concatenate([w_in[:, :a0], w_in[:, a0 + GLA_GATE_RANK:], a_cols], axis=1).astype(BF16)
    w_a2p = jnp.pad(w_a2, ((0, LANES - GLA_GATE_RANK), (0, 0))).astype(BF16)
    tok = lambda w: pl.BlockSpec((1, tm, w), lambda b_, s: (b_, s, 0))
    sd = lambda w, dt: jax.ShapeDtypeStruct((B, S, w), dt)
    return pl.pallas_call(
        _gla_proj_kernel,
        out_shape=(sd(hk, BF16), sd(hk, BF16), sd(hv, BF16), sd(hk, F32), sd(hv, BF16)),
        grid=(B, S // tm),
        in_specs=[tok(D), _const_spec(w_ext.shape), _const_spec(w_a2p.shape), _const_spec((1, hk))],
        out_specs=(tok(hk), tok(hk), tok(hv), tok(hk), tok(hv)),
        compiler_params=_params(2),
        name="gla_proj",
    )(x, w_ext, w_a2p, _row(b_a))


def _gla_tables(C):
    levels = [1 << e for e in range(int(math.log2(C)))][::-1]
    t = np.arange(C)
    u = np.arange(C)[None, :]
    blocks = []
    for s in levels:
        ref = (t // (2 * s)) * (2 * s) + s - 1
        upper = ((t // s) % 2) == 1
        q_role = (u > ref[:, None]) & (u <= t[:, None])
        k_role = (u > t[:, None]) & (u <= ref[:, None])
        blocks.append(np.where(upper[:, None], q_role, k_role))
    blocks.append(u <= t[:, None])
    blocks.append(u > t[:, None])
    sums = np.concatenate(blocks, axis=0).astype(np.float32)
    x = t[:, None] ^ t[None, :]
    lvl = np.full((C, C), -1, np.int32)
    for n, s in enumerate(levels):
        lvl = np.where((t[:, None] > t[None, :]) & (x >= s) & (x < 2 * s), n, lvl)
    lvl = np.where(t[:, None] == t[None, :], len(levels), lvl)
    return len(levels), sums, lvl.astype(np.int32)


def _gla_core_kernel(q_ref, k_ref, v_ref, la_ref, gate_ref, norm_ref, sums_ref, lvl_ref, o_ref, state_ref,
                     *, n_levels):
    C = GLA_CHUNK
    DK, DV = GLA_DK, GLA_DV

    @pl.when(pl.program_id(1) == 0)
    def _():
        state_ref[...] = jnp.zeros_like(state_ref)

    lvl = lvl_ref[...]
    for h in range(GLA_HEADS):
        q = q_ref[0, :, h * DK:(h + 1) * DK].astype(F32)
        k = k_ref[0, :, h * DK:(h + 1) * DK].astype(F32)
        qb = q_ref[0, :, h * DK:(h + 1) * DK]
        kb = k_ref[0, :, h * DK:(h + 1) * DK]
        v = v_ref[0, :, h * DV:(h + 1) * DV]
        la = la_ref[0, :, h * DK:(h + 1) * DK]
        hi = la.astype(BF16)
        lo = (la - hi.astype(F32)).astype(BF16)
        e2 = _dot(sums_ref[...], jnp.concatenate([hi, lo], axis=1))
        e = e2[:, :DK] + e2[:, DK:]

        a = jnp.where(lvl == n_levels, _dot_nt(qb, kb), 0.0)
        for n in range(n_levels):
            w = jnp.exp(e[n * C:(n + 1) * C, :])
            p = _dot_nt((q * w).astype(BF16), (k * w).astype(BF16))
            a = jnp.where(lvl == n, p, a)
        e_q = e[n_levels * C:(n_levels + 1) * C, :]
        e_k = e[(n_levels + 1) * C:(n_levels + 2) * C, :]
        st = state_ref[h]
        o = _dot_nt((q * jnp.exp(e_q)).astype(BF16), st.astype(BF16)) + _dot(a.astype(BF16), v)
        k_dec = (k * jnp.exp(e_k)).astype(BF16)
        state_ref[h] = jnp.exp(e_q[C - 1:C, :]) * st + _dot_tn(v, k_dec)
        o = _rmsnorm(o, norm_ref[...])
        o_ref[0, :, h * DV:(h + 1) * DV] = (o * gate_ref[0, :, h * DV:(h + 1) * DV].astype(F32)).astype(BF16)


def _gla_core(q, k, v, la, gate, out_norm):
    B, S, hk = q.shape
    hv = v.shape[-1]
    C = GLA_CHUNK
    n_levels, sums, lvl = _gla_tables(C)
    tok = lambda w: pl.BlockSpec((1, C, w), lambda b_, s: (b_, s, 0))
    return pl.pallas_call(
        functools.partial(_gla_core_kernel, n_levels=n_levels),
        out_shape=jax.ShapeDtypeStruct((B, S, hv), BF16),
        grid=(B, S // C),
        in_specs=[tok(hk), tok(hk), tok(hv), tok(hk), tok(hv), _const_spec((1, GLA_DV)),
                  _const_spec(sums.shape), _const_spec(lvl.shape)],
        out_specs=tok(hv),
        scratch_shapes=[pltpu.VMEM((GLA_HEADS, GLA_DV, GLA_DK), F32)],
        compiler_params=_params(2),
        name="gla_core",
    )(q, k, v, la, gate, _row(out_norm), jnp.asarray(sums, BF16), jnp.asarray(lvl))


def _conformer_kernel(x_ref, w_in_ref, b_in_ref, dw_ref, dwb_ref, lng_ref, lnb_ref, w_o_ref, b_o_ref,
                      g1_ref, b1_ref, o_ref, hist_ref, y_ref, *, tm):
    D = D_MODEL
    HALO = CONV_HALO

    @pl.when(pl.program_id(1) == 0)
    def _():
        hist_ref[0:HALO, :] = jnp.zeros((HALO, D), F32)

    x = x_ref[0]
    h = _dot(x.astype(BF16), w_in_ref[...]) + b_in_ref[...]
    hist_ref[HALO:HALO + tm, :] = h[:, :D] * _sigmoid(h[:, D:])

    n_m = (CONV_WIDTH + SUBLANES - 1) // SUBLANES
    for c0 in range(0, D, LANES):
        y = None
        for r in range(SUBLANES):
            z = None
            for m in range(n_m):
                d = SUBLANES * m + r
                if d >= CONV_WIDTH:
                    continue
                start = HALO - SUBLANES - SUBLANES * m
                tap = CONV_WIDTH - 1 - d
                term = dw_ref[tap:tap + 1, c0:c0 + LANES] * hist_ref[start:start + tm + SUBLANES, c0:c0 + LANES]
                z = term if z is None else z + term
            if r:
                z = pltpu.roll(z, r, 0)
            y = z if y is None else y + z
        y_ref[:, c0:c0 + LANES] = y[SUBLANES:, :] + dwb_ref[:, c0:c0 + LANES]
    hist_ref[0:HALO, :] = hist_ref[tm:tm + HALO, :]

    t = _layernorm(y_ref[...], lng_ref[...], lnb_ref[...])
    t = (t * _sigmoid(t)).astype(BF16)
    out = _dot(t, w_o_ref[...]) + b_o_ref[...]
    o_ref[0] = _layernorm(DEEPNORM_ALPHA * x + out, g1_ref[...], b1_ref[...])


def _conformer(x, w_in, b_in, dw, dw_b, ln_g, ln_b, w_o, b_o, g1, b1, tm):
    B, S, D = x.shape
    return pl.pallas_call(
        functools.partial(_conformer_kernel, tm=tm),
        out_shape=jax.ShapeDtypeStruct((B, S, D), F32),
        grid=(B, S // tm),
        in_specs=[pl.BlockSpec((1, tm, D), lambda b_, s: (b_, s, 0)),
                  _const_spec((D, 2 * D)), _const_spec((1, 2 * D)), _const_spec((CONV_WIDTH, D)),
                  _const_spec((1, D)), _const_spec((1, D)), _const_spec((1, D)), _const_spec((D, D)),
                  _const_spec((1, D)), _const_spec((1, D)), _const_spec((1, D))],
        out_specs=pl.BlockSpec((1, tm, D), lambda b_, s: (b_, s, 0)),
        scratch_shapes=[pltpu.VMEM((CONV_HALO + tm, D), F32), pltpu.VMEM((tm, D), F32)],
        compiler_params=_params(2),
        name="conformer_ln",
    )(x, w_in.astype(BF16), _row(b_in), dw.astype(F32), _row(dw_b), _row(ln_g), _row(ln_b),
      w_o.astype(BF16), _row(b_o), _row(g1), _row(b1))


def _tile(S, want):
    t = min(want, S)
    assert S % t == 0
    return t


def _mla_mix(x, rope, w_in, q_norm, kv_norm, w_uq, w_ukv):
    S = x.shape[1]
    q, kt, v = _mla_proj(x, rope[0], rope[1], w_in, q_norm, kv_norm, w_uq, w_ukv, _tile(S, MLA_KV_BLOCK))
    return _flash(q, kt, v, _tile(S, MLA_Q_BLOCK))


def _gla_mix(x, w_in, w_a2, b_a, out_norm):
    S = x.shape[1]
    q, k, v, la, gate = _gla_proj(x, w_in, w_a2, b_a, _tile(S, TOKEN_TILE))
    return _gla_core(q, k, v, la, gate, out_norm)


def kernel(x, positions, l0_mla_w_in, l0_mla_q_norm, l0_mla_kv_norm, l0_mla_w_uq, l0_mla_w_ukv, l0_mla_w_o, l0_ln1_g, l0_ln1_b, l0_ffn_w_in, l0_ffn_conv, l0_ffn_conv_b, l0_ffn_w_out, l0_ln2_g, l0_ln2_b, l1_gla_w_in, l1_gla_w_a2, l1_gla_b_a, l1_gla_out_norm, l1_gla_w_o, l1_ln1_g, l1_ln1_b, l1_ffn_w_in, l1_ffn_conv, l1_ffn_conv_b, l1_ffn_w_out, l1_ln2_g, l1_ln2_b, l2_conv_w_in, l2_conv_b_in, l2_conv_dw, l2_conv_dw_b, l2_conv_ln_g, l2_conv_ln_b, l2_conv_w_o, l2_conv_b_o, l2_ln1_g, l2_ln1_b, l2_ffn_w_in, l2_ffn_conv, l2_ffn_conv_b, l2_ffn_w_out, l2_ln2_g, l2_ln2_b, l3_mla_w_in, l3_mla_q_norm, l3_mla_kv_norm, l3_mla_w_uq, l3_mla_w_ukv, l3_mla_w_o, l3_ln1_g, l3_ln1_b, l3_ffn_w_in, l3_ffn_conv, l3_ffn_conv_b, l3_ffn_w_out, l3_ln2_g, l3_ln2_b):
    S = x.shape[1]
    assert S % GLA_CHUNK == 0 or S < GLA_CHUNK
    tm = _tile(S, TOKEN_TILE)
    rope = _rope_table(positions, tm)

    o = _mla_mix(x, rope, l0_mla_w_in, l0_mla_q_norm, l0_mla_kv_norm, l0_mla_w_uq, l0_mla_w_ukv)
    x = _ffn(x, l0_ffn_w_in, l0_ffn_conv, l0_ffn_conv_b, l0_ffn_w_out, l0_ln2_g, l0_ln2_b, tm,
             mixer=(o, l0_mla_w_o, l0_ln1_g, l0_ln1_b))

    o = _gla_mix(x, l1_gla_w_in, l1_gla_w_a2, l1_gla_b_a, l1_gla_out_norm)
    x = _ffn(x, l1_ffn_w_in, l1_ffn_conv, l1_ffn_conv_b, l1_ffn_w_out, l1_ln2_g, l1_ln2_b, tm,
             mixer=(o, l1_gla_w_o, l1_ln1_g, l1_ln1_b))

    x = _conformer(x, l2_conv_w_in, l2_conv_b_in, l2_conv_dw, l2_conv_dw_b, l2_conv_ln_g, l2_conv_ln_b,
                   l2_conv_w_o, l2_conv_b_o, l2_ln1_g, l2_ln1_b, _tile(S, CONFORMER_TILE))
    x = _ffn(x, l2_ffn_w_in, l2_ffn_conv, l2_ffn_conv_b, l2_ffn_w_out, l2_ln2_g, l2_ln2_b, tm)

    o = _mla_mix(x, rope, l3_mla_w_in, l3_mla_q_norm, l3_mla_kv_norm, l3_mla_w_uq, l3_mla_w_ukv)
    x = _ffn(x, l3_ffn_w_in, l3_ffn_conv, l3_ffn_conv_b, l3_ffn_w_out, l3_ln2_g, l3_ln2_b, tm,
             mixer=(o, l3_mla_w_o, l3_ln1_g, l3_ln1_b))
    return x
```

```python
import functools
import math

import numpy as np
import jax
import jax.numpy as jnp
from jax import lax
from jax.experimental import pallas as pl
from jax.experimental.pallas import tpu as pltpu

F32 = jnp.float32
BF16 = jnp.bfloat16

D_MODEL = 1024
DEPTH = 4
MLA_HEADS = 8
MLA_NOPE = 128
MLA_ROPE = 64
MLA_V = 128
MLA_Q_RANK = 256
MLA_KV_RANK = 128
MLA_QK_PAD = 256
MLA_KV_BLOCK = 512
MLA_Q_BLOCK = 2048
FLASH_UNROLL_PAIRS = 2
ROPE_THETA = 10000.0
GLA_HEADS = 4
GLA_DK = 128
GLA_DV = 256
GLA_GATE_RANK = 16
GLA_GATE_NORM = 16.0
GLA_CHUNK = 256
CONV_WIDTH = 31
CONV_HALO = 32
FFN_DIM = 2816
FFN_CONV_WIDTH = 3
FFN_CHUNKS = ((0, 1536), (1536, 2816))
TOKEN_TILE = 512
FFN_TILE = 1024
CONFORMER_TILE = 512
NORM_EPS = 1e-5
DEEPNORM_ALPHA = (2.0 * DEPTH) ** 0.25
SUBLANES = 8
LANES = 128
VMEM_LIMIT = 56 * 1024 * 1024
FFN_VMEM_LIMIT = 62 * 1024 * 1024


def _dot(a, b):
    return jnp.dot(a, b, preferred_element_type=F32)


def _dot_nt(a, b):
    return lax.dot_general(a, b, (((1,), (1,)), ((), ())), preferred_element_type=F32)


def _dot_tn(a, b):
    return lax.dot_general(a, b, (((0,), (0,)), ((), ())), preferred_element_type=F32)


def _layernorm(y, g, b):
    mu = jnp.mean(y, axis=-1, keepdims=True)
    d = y - mu
    var = jnp.mean(d * d, axis=-1, keepdims=True)
    return d * lax.rsqrt(var + NORM_EPS) * g + b


def _rmsnorm(y, g):
    return y * lax.rsqrt(jnp.mean(y * y, axis=-1, keepdims=True) + NORM_EPS) * g


def _sigmoid(x):
    return 1.0 / (1.0 + jnp.exp(-x))


def _const_spec(shape):
    nd = len(shape)
    return pl.BlockSpec(shape, lambda *_: (0,) * nd, pipeline_mode=pl.Buffered(1))


def _params(n_axes, vmem_limit=VMEM_LIMIT):
    return pltpu.CompilerParams(dimension_semantics=("arbitrary",) * n_axes, vmem_limit_bytes=vmem_limit)


def _row(v):
    return v.reshape(1, -1).astype(F32)


def _rope_table_kernel(pos_ref, freq_ref, cos_ref, sin_ref):
    ang = pos_ref[0] * freq_ref[...]
    cos_ref[0] = jnp.cos(ang)
    sin_ref[0] = jnp.sin(ang)


def _rope_table(positions, tm):
    B, S = positions.shape
    half = MLA_ROPE // 2
    inv_freq = ROPE_THETA ** (-jnp.arange(half, dtype=F32) / half)
    freq = jnp.tile(inv_freq, LANES // half).reshape(1, LANES)
    posf = positions.astype(F32).reshape(B, S, 1)
    out = jax.ShapeDtypeStruct((B, S, LANES), F32)
    return pl.pallas_call(
        _rope_table_kernel,
        out_shape=(out, out),
        grid=(B, S // tm),
        in_specs=[pl.BlockSpec((1, tm, 1), lambda b, s: (b, s, 0)), _const_spec((1, LANES))],
        out_specs=(pl.BlockSpec((1, tm, LANES), lambda b, s: (b, s, 0)),
                   pl.BlockSpec((1, tm, LANES), lambda b, s: (b, s, 0))),
        compiler_params=_params(2),
        name="rope_table",
    )(posf, freq)


def _mla_proj_kernel(x_ref, cos_ref, sin_ref, w_in_ref, qn_ref, kvn_ref, w_uq_ref, w_ukt_ref, w_uv_ref,
                     q_ref, kt_ref, v_ref):
    H = MLA_HEADS
    tm = x_ref.shape[1]
    x = x_ref[0].astype(BF16)
    c = _dot(x, w_in_ref[...])
    cos = cos_ref[0]
    sin = sin_ref[0]
    c_q = _rmsnorm(c[:, :MLA_Q_RANK], qn_ref[...]).astype(BF16)
    c_kv = _rmsnorm(c[:, MLA_Q_RANK:MLA_Q_RANK + MLA_KV_RANK], kvn_ref[...])
    o = MLA_Q_RANK + MLA_KV_RANK
    k_rope_t = (c[:, o:o + LANES] * cos + c[:, o + LANES:o + 2 * LANES] * sin).T.astype(BF16)
    scale = (MLA_NOPE + MLA_ROPE) ** -0.5 * math.log2(math.e)
    q_all = _dot(c_q, w_uq_ref[...]) * scale
    k_nope_t = _dot(w_ukt_ref[...], c_kv.T.astype(BF16))
    v = _dot(c_kv.astype(BF16), w_uv_ref[...])
    lane = lax.broadcasted_iota(jnp.int32, (tm, LANES), 1)
    ones_col = jnp.where(lane == 0, 1.0, 0.0).astype(BF16)
    for h in range(H):
        q_ref[0, h, :, 0:LANES] = q_all[:, h * LANES:(h + 1) * LANES].astype(BF16)
        qr = (q_all[:, (H + h) * LANES:(H + h + 1) * LANES] * cos
              + q_all[:, (2 * H + h) * LANES:(2 * H + h + 1) * LANES] * sin)
        q_ref[0, h, :, LANES:2 * LANES] = qr.astype(BF16)
        kt_ref[0, h, 0, 0:LANES, :] = k_nope_t[h * LANES:(h + 1) * LANES, :].astype(BF16)
        kt_ref[0, h, 0, LANES:2 * LANES, :] = k_rope_t
        v_ref[0, h, :, 0:LANES] = v[:, h * LANES:(h + 1) * LANES].astype(BF16)
        v_ref[0, h, :, LANES:2 * LANES] = ones_col


def _rot_half_cols(w):
    half = w.shape[1] // 2
    return jnp.concatenate([-w[:, half:], w[:, :half]], axis=1)


def _mla_proj(x, cos, sin, w_in, q_norm, kv_norm, w_uq, w_ukv, tm):
    B, S, D = x.shape
    H = MLA_HEADS
    o = MLA_Q_RANK + MLA_KV_RANK
    kr = w_in[:, o:]
    krot = _rot_half_cols(kr)
    w_in_ext = jnp.concatenate([w_in[:, :o], kr, kr, krot, krot], axis=1).astype(BF16)
    w_uq3 = w_uq.reshape(MLA_Q_RANK, H, MLA_NOPE + MLA_ROPE)
    nope = w_uq3[:, :, :MLA_NOPE].reshape(MLA_Q_RANK, H * MLA_NOPE)
    rope = w_uq3[:, :, MLA_NOPE:]
    rot = jnp.concatenate([-rope[:, :, MLA_ROPE // 2:], rope[:, :, :MLA_ROPE // 2]], axis=2)
    zpad = jnp.zeros((MLA_Q_RANK, H, LANES - MLA_ROPE), w_uq.dtype)
    rope_p = jnp.concatenate([rope, zpad], axis=2).reshape(MLA_Q_RANK, H * LANES)
    rot_p = jnp.concatenate([rot, zpad], axis=2).reshape(MLA_Q_RANK, H * LANES)
    w_uq_ext = jnp.concatenate([nope, rope_p, rot_p], axis=1).astype(BF16)
    w_ukv3 = w_ukv.reshape(MLA_KV_RANK, H, MLA_NOPE + MLA_V)
    w_ukt = w_ukv3[:, :, :MLA_NOPE].reshape(MLA_KV_RANK, H * MLA_NOPE).T.astype(BF16)
    w_uv = w_ukv3[:, :, MLA_NOPE:].reshape(MLA_KV_RANK, H * MLA_V).astype(BF16)
    qv = jax.ShapeDtypeStruct((B, H, S, MLA_QK_PAD), BF16)
    kt = jax.ShapeDtypeStruct((B, H, S // tm, MLA_QK_PAD, tm), BF16)
    tok = lambda w: pl.BlockSpec((1, tm, w), lambda b, s: (b, s, 0))
    hd = pl.BlockSpec((1, H, tm, MLA_QK_PAD), lambda b, s: (b, 0, s, 0))
    return pl.pallas_call(
        _mla_proj_kernel,
        out_shape=(qv, kt, qv),
        grid=(B, S // tm),
        in_specs=[tok(D), tok(LANES), tok(LANES), _const_spec(w_in_ext.shape), _const_spec((1, MLA_Q_RANK)),
                  _const_spec((1, MLA_KV_RANK)), _const_spec(w_uq_ext.shape), _const_spec(w_ukt.shape),
                  _const_spec(w_uv.shape)],
        out_specs=(hd, pl.BlockSpec((1, H, 1, MLA_QK_PAD, tm), lambda b, s: (b, 0, s, 0, 0)), hd),
        compiler_params=_params(2),
        name="mla_proj",
    )(x, cos, sin, w_in_ext, _row(q_norm), _row(kv_norm), w_uq_ext, w_ukt, w_uv)


def _flash_kernel(q_ref, kt_ref, v_ref, o_ref, sa_ref, sb_ref, *, bq, bk):
    qi = pl.program_id(2)
    n_sub = bq // bk
    q = q_ref[0, 0]

    def scores(q_rows, j):
        return _dot(q_rows, kt_ref[0, 0, j])

    def update(s, j, m, acc, masked):
        start = pl.multiple_of(j * bk, bk)
        v = v_ref[0, 0, pl.ds(start, bk), :]
        if masked:
            row = lax.broadcasted_iota(jnp.int32, s.shape, 0)
            col = lax.broadcasted_iota(jnp.int32, s.shape, 1)
            s = jnp.where(col <= row, s, -jnp.inf)
        m_new = jnp.maximum(m, jnp.max(s, axis=-1, keepdims=True))
        p = jnp.exp2(s - m_new)
        acc = jnp.exp2(m - m_new) * acc + _dot(p.astype(BF16), v)
        return m_new, acc

    def pairs(n_pairs, base):
        def body(t, carry):
            m, acc = carry
            j = base + 2 * n_pairs * t
            for u in range(n_pairs):
                sb_ref[...] = scores(q, j + 2 * u + 1)
                m, acc = update(sa_ref[...], j + 2 * u, m, acc, False)
                sa_ref[...] = scores(q, j + 2 * u + 2)
                m, acc = update(sb_ref[...], j + 2 * u + 1, m, acc, False)
            return m, acc
        return body

    n_off = n_sub * qi
    m = jnp.full((bq, 1), -jnp.inf, F32)
    acc = jnp.zeros((bq, 2 * MLA_V), F32)
    sa_ref[...] = scores(q, 0)
    n_long = n_off // (2 * FLASH_UNROLL_PAIRS)
    m, acc = lax.fori_loop(0, n_long, pairs(FLASH_UNROLL_PAIRS, 0), (m, acc))
    if n_sub % (2 * FLASH_UNROLL_PAIRS):
        done = 2 * FLASH_UNROLL_PAIRS * n_long
        m, acc = lax.fori_loop(0, (n_off - done) // 2, pairs(1, done), (m, acc))
    s = sa_ref[...]
    for t in range(n_sub):
        r0 = t * bk
        s_next = scores(q[r0 + bk:], n_off + t + 1) if t + 1 < n_sub else None
        m_t, acc_t = update(s, n_off + t, m[r0:], acc[r0:], True)
        m = m_t if t == 0 else jnp.concatenate([m[:r0], m_t], axis=0)
        acc = acc_t if t == 0 else jnp.concatenate([acc[:r0], acc_t], axis=0)
        s = s_next
    o_ref[0] = (acc[:, :MLA_V] / acc[:, MLA_V:MLA_V + 1]).astype(o_ref.dtype)


def _flash(q, kt, v, bq):
    B, H, S, _ = q.shape
    n_kv, _, bk = kt.shape[2:]
    assert (bq // bk) % 2 == 0 or S == bq
    return pl.pallas_call(
        functools.partial(_flash_kernel, bq=bq, bk=bk),
        out_shape=jax.ShapeDtypeStruct((B, S, H * MLA_V), BF16),
        grid=(B, H, S // bq),
        in_specs=[pl.BlockSpec((1, 1, bq, MLA_QK_PAD), lambda b, h, i: (b, h, i, 0)),
                  pl.BlockSpec((1, 1, n_kv, MLA_QK_PAD, bk), lambda b, h, i: (b, h, 0, 0, 0)),
                  pl.BlockSpec((1, 1, S, MLA_QK_PAD), lambda b, h, i: (b, h, 0, 0))],
        out_specs=pl.BlockSpec((1, bq, MLA_V), lambda b, h, i: (b, i, h)),
        scratch_shapes=[pltpu.VMEM((bq, bk), F32), pltpu.VMEM((bq, bk), F32)],
        compiler_params=_params(3),
        name="mla_flash",
    )(q, kt, v)


def _ffn_body(x, w_in_ref, cw_ref, cb_ref, w_out_ref, g_ref, b_ref, o_ref, halo_ref):
    tm = x.shape[0]

    @pl.when(pl.program_id(1) == 0)
    def _():
        halo_ref[...] = jnp.zeros_like(halo_ref)

    xb = x.astype(BF16)

    def conv(c0, c1):
        u = _dot(xb, w_in_ref[:, c0:c1])
        ext = jnp.concatenate([halo_ref[:, c0:c1], u], axis=0)
        halo_ref[:, c0:c1] = u[tm - SUBLANES:, :]
        s1 = pltpu.roll(ext, 1, 0)[SUBLANES:, :]
        s2 = pltpu.roll(ext, 2, 0)[SUBLANES:, :]
        w = cw_ref[:, c0:c1]
        return w[2:3, :] * u + w[1:2, :] * s1 + w[0:1, :] * s2 + cb_ref[:, c0:c1]

    acc = jnp.zeros((tm, D_MODEL), F32)
    for c0, c1 in FFN_CHUNKS:
        gate = conv(c0, c1)
        val = conv(FFN_DIM + c0, FFN_DIM + c1)
        act = (gate * _sigmoid(gate) * val).astype(BF16)
        acc = acc + _dot(act, w_out_ref[c0:c1, :])
    o_ref[0] = _layernorm(DEEPNORM_ALPHA * x + acc, g_ref[...], b_ref[...])


def _ffn_kernel(x_ref, *rest):
    _ffn_body(x_ref[0], *rest)


def _mix_ffn_kernel(a_ref, x_ref, w_o_ref, g1_ref, b1_ref, *rest):
    h = _dot(a_ref[0], w_o_ref[...])
    _ffn_body(_layernorm(DEEPNORM_ALPHA * x_ref[0] + h, g1_ref[...], b1_ref[...]), *rest)


def _ffn(x, w_in, conv_w, conv_b, w_out, g, b, tm, mixer=None):
    B, S, D = x.shape
    tok = lambda w: pl.BlockSpec((1, tm, w), lambda b_, s: (b_, s, 0))
    args = [x, w_in.astype(BF16), conv_w.astype(F32), _row(conv_b), w_out.astype(BF16), _row(g), _row(b)]
    specs = [tok(D), _const_spec((D, 2 * FFN_DIM)), _const_spec((FFN_CONV_WIDTH, 2 * FFN_DIM)),
             _const_spec((1, 2 * FFN_DIM)), _const_spec((FFN_DIM, D)), _const_spec((1, D)), _const_spec((1, D))]
    body = _ffn_kernel
    if mixer is not None:
        a, w_o, g1, b1 = mixer
        K = a.shape[-1]
        args = [a, x, w_o.astype(BF16), _row(g1), _row(b1)] + args[1:]
        specs = [tok(K), tok(D), _const_spec((K, D)), _const_spec((1, D)), _const_spec((1, D))] + specs[1:]
        body = _mix_ffn_kernel
    return pl.pallas_call(
        body,
        out_shape=jax.ShapeDtypeStruct((B, S, D), F32),
        grid=(B, S // tm),
        in_specs=specs,
        out_specs=tok(D),
        scratch_shapes=[pltpu.VMEM((SUBLANES, 2 * FFN_DIM), F32)],
        compiler_params=_params(2, FFN_VMEM_LIMIT),
        name="conv_ffn_ln",
    )(*args)


def _gla_proj_kernel(x_ref, w_in_ref, w_a2_ref, b_a_ref, q_ref, k_ref, v_ref, la_ref, gate_ref):
    hk = GLA_HEADS * GLA_DK
    hv = GLA_HEADS * GLA_DV
    proj = _dot(x_ref[0].astype(BF16), w_in_ref[...])
    q_ref[0] = (proj[:, :hk] * GLA_DK ** -0.5).astype(BF16)
    k_ref[0] = proj[:, hk:2 * hk].astype(BF16)
    v_ref[0] = proj[:, 2 * hk:2 * hk + hv].astype(BF16)
    r = proj[:, 2 * hk + hv:2 * hk + 2 * hv]
    gate_ref[0] = (r * _sigmoid(r)).astype(BF16)
    a_lr = proj[:, 2 * hk + 2 * hv:].astype(BF16)
    z = _dot(a_lr, w_a2_ref[...]) + b_a_ref[...]
    la_ref[0] = (jnp.minimum(z, 0.0) - jnp.log(1.0 + jnp.exp(-jnp.abs(z)))) * (1.0 / GLA_GATE_NORM)


def _gla_proj(x, w_in, w_a2, b_a, tm):
    B, S, D = x.shape
    hk = GLA_HEADS * GLA_DK
    hv = GLA_HEADS * GLA_DV
    a0 = 2 * hk + hv
    a_cols = jnp.pad(w_in[:, a0:a0 + GLA_GATE_RANK], ((0, 0), (0, LANES - GLA_GATE_RANK)))
    w_ext = jnp.concatenate([w_in[:, :a0], w_in[:, a0 + GLA_GATE_RANK:], a_cols], axis=1).astype(BF16)
    w_a2p = jnp.pad(w_a2, ((0, LANES - GLA_GATE_RANK), (0, 0))).astype(BF16)
    tok = lambda w: pl.BlockSpec((1, tm, w), lambda b_, s: (b_, s, 0))
    sd = lambda w, dt: jax.ShapeDtypeStruct((B, S, w), dt)
    return pl.pallas_call(
        _gla_proj_kernel,
        out_shape=(sd(hk, BF16), sd(hk, BF16), sd(hv, BF16), sd(hk, F32), sd(hv, BF16)),
        grid=(B, S // tm),
        in_specs=[tok(D), _const_spec(w_ext.shape), _const_spec(w_a2p.shape), _const_spec((1, hk))],
        out_specs=(tok(hk), tok(hk), tok(hv), tok(hk), tok(hv)),
        compiler_params=_params(2),
        name="gla_proj",
    )(x, w_ext, w_a2p, _row(b_a))


def _gla_tables(C):
    levels = [1 << e for e in range(int(math.log2(C)))][::-1]
    t = np.arange(C)
    u = np.arange(C)[None, :]
    blocks = []
    for s in levels:
        ref = (t // (2 * s)) * (2 * s) + s - 1
        upper = ((t // s) % 2) == 1
        q_role = (u > ref[:, None]) & (u <= t[:, None])
        k_role = (u > t[:, None]) & (u <= ref[:, None])
        blocks.append(np.where(upper[:, None], q_role, k_role))
    blocks.append(u <= t[:, None])
    blocks.append(u > t[:, None])
    sums = np.concatenate(blocks, axis=0).astype(np.float32)
    x = t[:, None] ^ t[None, :]
    lvl = np.full((C, C), -1, np.int32)
    for n, s in enumerate(levels):
        lvl = np.where((t[:, None] > t[None, :]) & (x >= s) & (x < 2 * s), n, lvl)
    lvl = np.where(t[:, None] == t[None, :], len(levels), lvl)
    return len(levels), sums, lvl.astype(np.int32)


def _gla_core_kernel(q_ref, k_ref, v_ref, la_ref, gate_ref, norm_ref, sums_ref, lvl_ref, o_ref, state_ref,
                     *, n_levels):
    C = GLA_CHUNK
    DK, DV = GLA_DK, GLA_DV

    @pl.when(pl.program_id(1) == 0)
    def _():
        state_ref[...] = jnp.zeros_like(state_ref)

    lvl = lvl_ref[...]
    for h in range(GLA_HEADS):
        q = q_ref[0, :, h * DK:(h + 1) * DK].astype(F32)
        k = k_ref[0, :, h * DK:(h + 1) * DK].astype(F32)
        qb = q_ref[0, :, h * DK:(h + 1) * DK]
        kb = k_ref[0, :, h * DK:(h + 1) * DK]
        v = v_ref[0, :, h * DV:(h + 1) * DV]
        la = la_ref[0, :, h * DK:(h + 1) * DK]
        hi = la.astype(BF16)
        lo = (la - hi.astype(F32)).astype(BF16)
        e2 = _dot(sums_ref[...], jnp.concatenate([hi, lo], axis=1))
        e = e2[:, :DK] + e2[:, DK:]

        a = jnp.where(lvl == n_levels, _dot_nt(qb, kb), 0.0)
        for n in range(n_levels):
            w = jnp.exp(e[n * C:(n + 1) * C, :])
            p = _dot_nt((q * w).astype(BF16), (k * w).astype(BF16))
            a = jnp.where(lvl == n, p, a)
        e_q = e[n_levels * C:(n_levels + 1) * C, :]
        e_k = e[(n_levels + 1) * C:(n_levels + 2) * C, :]
        st = state_ref[h]
        o = _dot_nt((q * jnp.exp(e_q)).astype(BF16), st.astype(BF16)) + _dot(a.astype(BF16), v)
        k_dec = (k * jnp.exp(e_k)).astype(BF16)
        state_ref[h] = jnp.exp(e_q[C - 1:C, :]) * st + _dot_tn(v, k_dec)
        o = _rmsnorm(o, norm_ref[...])
        o_ref[0, :, h * DV:(h + 1) * DV] = (o * gate_ref[0, :, h * DV:(h + 1) * DV].astype(F32)).astype(BF16)


def _gla_core(q, k, v, la, gate, out_norm):
    B, S, hk = q.shape
    hv = v.shape[-1]
    C = GLA_CHUNK
    n_levels, sums, lvl = _gla_tables(C)
    tok = lambda w: pl.BlockSpec((1, C, w), lambda b_, s: (b_, s, 0))
    return pl.pallas_call(
        functools.partial(_gla_core_kernel, n_levels=n_levels),
        out_shape=jax.ShapeDtypeStruct((B, S, hv), BF16),
        grid=(B, S // C),
        in_specs=[tok(hk), tok(hk), tok(hv), tok(hk), tok(hv), _const_spec((1, GLA_DV)),
                  _const_spec(sums.shape), _const_spec(lvl.shape)],
        out_specs=tok(hv),
        scratch_shapes=[pltpu.VMEM((GLA_HEADS, GLA_DV, GLA_DK), F32)],
        compiler_params=_params(2),
        name="gla_core",
    )(q, k, v, la, gate, _row(out_norm), jnp.asarray(sums, BF16), jnp.asarray(lvl))


def _conformer_kernel(x_ref, w_in_ref, b_in_ref, dw_ref, dwb_ref, lng_ref, lnb_ref, w_o_ref, b_o_ref,
                      g1_ref, b1_ref, o_ref, hist_ref, y_ref, *, tm):
    D = D_MODEL
    HALO = CONV_HALO

    @pl.when(pl.program_id(1) == 0)
    def _():
        hist_ref[0:HALO, :] = jnp.zeros((HALO, D), F32)

    x = x_ref[0]
    h = _dot(x.astype(BF16), w_in_ref[...]) + b_in_ref[...]
    hist_ref[HALO:HALO + tm, :] = h[:, :D] * _sigmoid(h[:, D:])

    n_m = (CONV_WIDTH + SUBLANES - 1) // SUBLANES
    for c0 in range(0, D, LANES):
        y = None
        for r in range(SUBLANES):
            z = None
            for m in range(n_m):
                d = SUBLANES * m + r
                if d >= CONV_WIDTH:
                    continue
                start = HALO - SUBLANES - SUBLANES * m
                tap = CONV_WIDTH - 1 - d
                term = dw_ref[tap:tap + 1, c0:c0 + LANES] * hist_ref[start:start + tm + SUBLANES, c0:c0 + LANES]
                z = term if z is None else z + term
            if r:
                z = pltpu.roll(z, r, 0)
            y = z if y is None else y + z
        y_ref[:, c0:c0 + LANES] = y[SUBLANES:, :] + dwb_ref[:, c0:c0 + LANES]
    hist_ref[0:HALO, :] = hist_ref[tm:tm + HALO, :]

    t = _layernorm(y_ref[...], lng_ref[...], lnb_ref[...])
    t = (t * _sigmoid(t)).astype(BF16)
    out = _dot(t, w_o_ref[...]) + b_o_ref[...]
    o_ref[0] = _layernorm(DEEPNORM_ALPHA * x + out, g1_ref[...], b1_ref[...])


def _conformer(x, w_in, b_in, dw, dw_b, ln_g, ln_b, w_o, b_o, g1, b1, tm):
    B, S, D = x.shape
    return pl.pallas_call(
        functools.partial(_conformer_kernel, tm=tm),
        out_shape=jax.ShapeDtypeStruct((B, S, D), F32),
        grid=(B, S // tm),
        in_specs=[pl.BlockSpec((1, tm, D), lambda b_, s: (b_, s, 0)),
                  _const_spec((D, 2 * D)), _const_spec((1, 2 * D)), _const_spec((CONV_WIDTH, D)),
                  _const_spec((1, D)), _const_spec((1, D)), _const_spec((1, D)), _const_spec((D, D)),
                  _const_spec((1, D)), _const_spec((1, D)), _const_spec((1, D))],
        out_specs=pl.BlockSpec((1, tm, D), lambda b_, s: (b_, s, 0)),
        scratch_shapes=[pltpu.VMEM((CONV_HALO + tm, D), F32), pltpu.VMEM((tm, D), F32)],
        compiler_params=_params(2),
        name="conformer_ln",
    )(x, w_in.astype(BF16), _row(b_in), dw.astype(F32), _row(dw_b), _row(ln_g), _row(ln_b),
      w_o.astype(BF16), _row(b_o), _row(g1), _row(b1))


def _tile(S, want):
    t = min(want, S)
    assert S % t == 0
    return t


def _mla_mix(x, rope, w_in, q_norm, kv_norm, w_uq, w_ukv):
    S = x.shape[1]
    q, kt, v = _mla_proj(x, rope[0], rope[1], w_in, q_norm, kv_norm, w_uq, w_ukv, _tile(S, MLA_KV_BLOCK))
    return _flash(q, kt, v, _tile(S, MLA_Q_BLOCK))


def _gla_mix(x, w_in, w_a2, b_a, out_norm):
    S = x.shape[1]
    q, k, v, la, gate = _gla_proj(x, w_in, w_a2, b_a, _tile(S, 2 * TOKEN_TILE))
    return _gla_core(q, k, v, la, gate, out_norm)


def kernel(x, positions, l0_mla_w_in, l0_mla_q_norm, l0_mla_kv_norm, l0_mla_w_uq, l0_mla_w_ukv, l0_mla_w_o, l0_ln1_g, l0_ln1_b, l0_ffn_w_in, l0_ffn_conv, l0_ffn_conv_b, l0_ffn_w_out, l0_ln2_g, l0_ln2_b, l1_gla_w_in, l1_gla_w_a2, l1_gla_b_a, l1_gla_out_norm, l1_gla_w_o, l1_ln1_g, l1_ln1_b, l1_ffn_w_in, l1_ffn_conv, l1_ffn_conv_b, l1_ffn_w_out, l1_ln2_g, l1_ln2_b, l2_conv_w_in, l2_conv_b_in, l2_conv_dw, l2_conv_dw_b, l2_conv_ln_g, l2_conv_ln_b, l2_conv_w_o, l2_conv_b_o, l2_ln1_g, l2_ln1_b, l2_ffn_w_in, l2_ffn_conv, l2_ffn_conv_b, l2_ffn_w_out, l2_ln2_g, l2_ln2_b, l3_mla_w_in, l3_mla_q_norm, l3_mla_kv_norm, l3_mla_w_uq, l3_mla_w_ukv, l3_mla_w_o, l3_ln1_g, l3_ln1_b, l3_ffn_w_in, l3_ffn_conv, l3_ffn_conv_b, l3_ffn_w_out, l3_ln2_g, l3_ln2_b):
    S = x.shape[1]
    assert S % GLA_CHUNK == 0 or S < GLA_CHUNK
    tm = _tile(S, FFN_TILE)
    rope = _rope_table(positions, _tile(S, TOKEN_TILE))

    o = _mla_mix(x, rope, l0_mla_w_in, l0_mla_q_norm, l0_mla_kv_norm, l0_mla_w_uq, l0_mla_w_ukv)
    x = _ffn(x, l0_ffn_w_in, l0_ffn_conv, l0_ffn_conv_b, l0_ffn_w_out, l0_ln2_g, l0_ln2_b, tm,
             mixer=(o, l0_mla_w_o, l0_ln1_g, l0_ln1_b))

    o = _gla_mix(x, l1_gla_w_in, l1_gla_w_a2, l1_gla_b_a, l1_gla_out_norm)
    x = _ffn(x, l1_ffn_w_in, l1_ffn_conv, l1_ffn_conv_b, l1_ffn_w_out, l1_ln2_g, l1_ln2_b, tm,
             mixer=(o, l1_gla_w_o, l1_ln1_g, l1_ln1_b))

    x = _conformer(x, l2_conv_w_in, l2_conv_b_in, l2_conv_dw, l2_conv_dw_b, l2_conv_ln_g, l2_conv_ln_b,
                   l2_conv_w_o, l2_conv_b_o, l2_ln1_g, l2_ln1_b, _tile(S, CONFORMER_TILE))
    x = _ffn(x, l2_ffn_w_in, l2_ffn_conv, l2_ffn_conv_b, l2_ffn_w_out, l2_ln2_g, l2_ln2_b, tm)

    o = _mla_mix(x, rope, l3_mla_w_in, l3_mla_q_norm, l3_mla_kv_norm, l3_mla_w_uq, l3_mla_w_ukv)
    x = _ffn(x, l3_ffn_w_in, l3_ffn_conv, l3_ffn_conv_b, l3_ffn_w_out, l3_ln2_g, l3_ln2_b, tm,
             mixer=(o, l3_mla_w_o, l3_ln1_g, l3_ln1_b))
    return x
```

```python
import functools
import math

import numpy as np
import jax
import jax.numpy as jnp
from jax import lax
from jax.experimental import pallas as pl
from jax.experimental.pallas import tpu as pltpu

F32 = jnp.float32
BF16 = jnp.bfloat16

D_MODEL = 1024
DEPTH = 4
MLA_HEADS = 8
MLA_NOPE = 128
MLA_ROPE = 64
MLA_V = 128
MLA_Q_RANK = 256
MLA_KV_RANK = 128
MLA_QK_PAD = 256
MLA_KV_BLOCK = 512
MLA_Q_BLOCK = 2048
FLASH_UNROLL_PAIRS = 2
ROPE_THETA = 10000.0
GLA_HEADS = 4
GLA_DK = 128
GLA_DV = 256
GLA_GATE_RANK = 16
GLA_GATE_NORM = 16.0
GLA_CHUNK = 256
CONV_WIDTH = 31
CONV_HALO = 32
FFN_DIM = 2816
FFN_CONV_WIDTH = 3
FFN_CHUNKS = ((0, 1536), (1536, 2816))
TOKEN_TILE = 512
FFN_TILE = 1024
CONFORMER_TILE = 512
NORM_EPS = 1e-5
DEEPNORM_ALPHA = (2.0 * DEPTH) ** 0.25
SUBLANES = 8
LANES = 128
VMEM_LIMIT = 56 * 1024 * 1024
FFN_VMEM_LIMIT = 62 * 1024 * 1024


def _dot(a, b):
    return jnp.dot(a, b, preferred_element_type=F32)


def _dot_nt(a, b):
    return lax.dot_general(a, b, (((1,), (1,)), ((), ())), preferred_element_type=F32)


def _dot_tn(a, b):
    return lax.dot_general(a, b, (((0,), (0,)), ((), ())), preferred_element_type=F32)


def _layernorm(y, g, b):
    mu = jnp.mean(y, axis=-1, keepdims=True)
    d = y - mu
    var = jnp.mean(d * d, axis=-1, keepdims=True)
    return d * lax.rsqrt(var + NORM_EPS) * g + b


def _rmsnorm(y, g):
    return y * lax.rsqrt(jnp.mean(y * y, axis=-1, keepdims=True) + NORM_EPS) * g


def _sigmoid(x):
    return 1.0 / (1.0 + jnp.exp(-x))


def _const_spec(shape):
    nd = len(shape)
    return pl.BlockSpec(shape, lambda *_: (0,) * nd, pipeline_mode=pl.Buffered(1))


def _params(n_axes, vmem_limit=VMEM_LIMIT):
    return pltpu.CompilerParams(dimension_semantics=("arbitrary",) * n_axes, vmem_limit_bytes=vmem_limit)


def _row(v):
    return v.reshape(1, -1).astype(F32)


def _rope_table_kernel(pos_ref, freq_ref, cos_ref, sin_ref):
    ang = pos_ref[0] * freq_ref[...]
    cos_ref[0] = jnp.cos(ang)
    sin_ref[0] = jnp.sin(ang)


def _rope_table(positions, tm):
    B, S = positions.shape
    half = MLA_ROPE // 2
    inv_freq = ROPE_THETA ** (-jnp.arange(half, dtype=F32) / half)
    freq = jnp.tile(inv_freq, LANES // half).reshape(1, LANES)
    posf = positions.astype(F32).reshape(B, S, 1)
    out = jax.ShapeDtypeStruct((B, S, LANES), F32)
    return pl.pallas_call(
        _rope_table_kernel,
        out_shape=(out, out),
        grid=(B, S // tm),
        in_specs=[pl.BlockSpec((1, tm, 1), lambda b, s: (b, s, 0)), _const_spec((1, LANES))],
        out_specs=(pl.BlockSpec((1, tm, LANES), lambda b, s: (b, s, 0)),
                   pl.BlockSpec((1, tm, LANES), lambda b, s: (b, s, 0))),
        compiler_params=_params(2),
        name="rope_table",
    )(posf, freq)


def _mla_proj_kernel(x_ref, cos_ref, sin_ref, w_in_ref, qn_ref, kvn_ref, w_uq_ref, w_ukt_ref, w_uv_ref,
                     q_ref, kt_ref, v_ref):
    H = MLA_HEADS
    tm = x_ref.shape[1]
    bk = kt_ref.shape[-1]
    x = x_ref[0].astype(BF16)
    c = _dot(x, w_in_ref[...])
    cos = cos_ref[0]
    sin = sin_ref[0]
    c_q = _rmsnorm(c[:, :MLA_Q_RANK], qn_ref[...]).astype(BF16)
    c_kv = _rmsnorm(c[:, MLA_Q_RANK:MLA_Q_RANK + MLA_KV_RANK], kvn_ref[...])
    o = MLA_Q_RANK + MLA_KV_RANK
    k_rope_t = (c[:, o:o + LANES] * cos + c[:, o + LANES:o + 2 * LANES] * sin).T.astype(BF16)
    scale = (MLA_NOPE + MLA_ROPE) ** -0.5 * math.log2(math.e)
    q_all = _dot(c_q, w_uq_ref[...]) * scale
    k_nope_t = _dot(w_ukt_ref[...], c_kv.T.astype(BF16))
    v = _dot(c_kv.astype(BF16), w_uv_ref[...])
    lane = lax.broadcasted_iota(jnp.int32, (tm, LANES), 1)
    ones_col = jnp.where(lane == 0, 1.0, 0.0).astype(BF16)
    for h in range(H):
        q_ref[0, h, :, 0:LANES] = q_all[:, h * LANES:(h + 1) * LANES].astype(BF16)
        qr = (q_all[:, (H + h) * LANES:(H + h + 1) * LANES] * cos
              + q_all[:, (2 * H + h) * LANES:(2 * H + h + 1) * LANES] * sin)
        q_ref[0, h, :, LANES:2 * LANES] = qr.astype(BF16)
        for blk in range(tm // bk):
            cols = slice(blk * bk, (blk + 1) * bk)
            kt_ref[0, h, blk, 0:LANES, :] = k_nope_t[h * LANES:(h + 1) * LANES, cols].astype(BF16)
            kt_ref[0, h, blk, LANES:2 * LANES, :] = k_rope_t[:, cols]
        v_ref[0, h, :, 0:LANES] = v[:, h * LANES:(h + 1) * LANES].astype(BF16)
        v_ref[0, h, :, LANES:2 * LANES] = ones_col


def _rot_half_cols(w):
    half = w.shape[1] // 2
    return jnp.concatenate([-w[:, half:], w[:, :half]], axis=1)


def _mla_proj(x, cos, sin, w_in, q_norm, kv_norm, w_uq, w_ukv, tm):
    B, S, D = x.shape
    H = MLA_HEADS
    o = MLA_Q_RANK + MLA_KV_RANK
    kr = w_in[:, o:]
    krot = _rot_half_cols(kr)
    w_in_ext = jnp.concatenate([w_in[:, :o], kr, kr, krot, krot], axis=1).astype(BF16)
    w_uq3 = w_uq.reshape(MLA_Q_RANK, H, MLA_NOPE + MLA_ROPE)
    nope = w_uq3[:, :, :MLA_NOPE].reshape(MLA_Q_RANK, H * MLA_NOPE)
    rope = w_uq3[:, :, MLA_NOPE:]
    rot = jnp.concatenate([-rope[:, :, MLA_ROPE // 2:], rope[:, :, :MLA_ROPE // 2]], axis=2)
    zpad = jnp.zeros((MLA_Q_RANK, H, LANES - MLA_ROPE), w_uq.dtype)
    rope_p = jnp.concatenate([rope, zpad], axis=2).reshape(MLA_Q_RANK, H * LANES)
    rot_p = jnp.concatenate([rot, zpad], axis=2).reshape(MLA_Q_RANK, H * LANES)
    w_uq_ext = jnp.concatenate([nope, rope_p, rot_p], axis=1).astype(BF16)
    w_ukv3 = w_ukv.reshape(MLA_KV_RANK, H, MLA_NOPE + MLA_V)
    w_ukt = w_ukv3[:, :, :MLA_NOPE].reshape(MLA_KV_RANK, H * MLA_NOPE).T.astype(BF16)
    w_uv = w_ukv3[:, :, MLA_NOPE:].reshape(MLA_KV_RANK, H * MLA_V).astype(BF16)
    qv = jax.ShapeDtypeStruct((B, H, S, MLA_QK_PAD), BF16)
    bk = min(tm, MLA_KV_BLOCK)
    kt = jax.ShapeDtypeStruct((B, H, S // bk, MLA_QK_PAD, bk), BF16)
    tok = lambda w: pl.BlockSpec((1, tm, w), lambda b, s: (b, s, 0))
    hd = pl.BlockSpec((1, H, tm, MLA_QK_PAD), lambda b, s: (b, 0, s, 0))
    return pl.pallas_call(
        _mla_proj_kernel,
        out_shape=(qv, kt, qv),
        grid=(B, S // tm),
        in_specs=[tok(D), tok(LANES), tok(LANES), _const_spec(w_in_ext.shape), _const_spec((1, MLA_Q_RANK)),
                  _const_spec((1, MLA_KV_RANK)), _const_spec(w_uq_ext.shape), _const_spec(w_ukt.shape),
                  _const_spec(w_uv.shape)],
        out_specs=(hd, pl.BlockSpec((1, H, tm // bk, MLA_QK_PAD, bk), lambda b, s: (b, 0, s, 0, 0)), hd),
        compiler_params=_params(2),
        name="mla_proj",
    )(x, cos, sin, w_in_ext, _row(q_norm), _row(kv_norm), w_uq_ext, w_ukt, w_uv)


def _flash_kernel(q_ref, kt_ref, v_ref, o_ref, sa_ref, sb_ref, *, bq, bk):
    qi = pl.program_id(2)
    n_sub = bq // bk
    q = q_ref[0, 0]

    def scores(q_rows, j):
        return _dot(q_rows, kt_ref[0, 0, j])

    def update(s, j, m, acc, masked):
        start = pl.multiple_of(j * bk, bk)
        v = v_ref[0, 0, pl.ds(start, bk), :]
        if masked:
            row = lax.broadcasted_iota(jnp.int32, s.shape, 0)
            col = lax.broadcasted_iota(jnp.int32, s.shape, 1)
            s = jnp.where(col <= row, s, -jnp.inf)
        m_new = jnp.maximum(m, jnp.max(s, axis=-1, keepdims=True))
        p = jnp.exp2(s - m_new)
        acc = jnp.exp2(m - m_new) * acc + _dot(p.astype(BF16), v)
        return m_new, acc

    def pairs(n_pairs, base):
        def body(t, carry):
            m, acc = carry
            j = base + 2 * n_pairs * t
            for u in range(n_pairs):
                sb_ref[...] = scores(q, j + 2 * u + 1)
                m, acc = update(sa_ref[...], j + 2 * u, m, acc, False)
                sa_ref[...] = scores(q, j + 2 * u + 2)
                m, acc = update(sb_ref[...], j + 2 * u + 1, m, acc, False)
            return m, acc
        return body

    n_off = n_sub * qi
    m = jnp.full((bq, 1), -jnp.inf, F32)
    acc = jnp.zeros((bq, 2 * MLA_V), F32)
    sa_ref[...] = scores(q, 0)
    n_long = n_off // (2 * FLASH_UNROLL_PAIRS)
    m, acc = lax.fori_loop(0, n_long, pairs(FLASH_UNROLL_PAIRS, 0), (m, acc))
    if n_sub % (2 * FLASH_UNROLL_PAIRS):
        done = 2 * FLASH_UNROLL_PAIRS * n_long
        m, acc = lax.fori_loop(0, (n_off - done) // 2, pairs(1, done), (m, acc))
    s = sa_ref[...]
    for t in range(n_sub):
        r0 = t * bk
        s_next = scores(q[r0 + bk:], n_off + t + 1) if t + 1 < n_sub else None
        m_t, acc_t = update(s, n_off + t, m[r0:], acc[r0:], True)
        m = m_t if t == 0 else jnp.concatenate([m[:r0], m_t], axis=0)
        acc = acc_t if t == 0 else jnp.concatenate([acc[:r0], acc_t], axis=0)
        s = s_next
    o_ref[0] = (acc[:, :MLA_V] / acc[:, MLA_V:MLA_V + 1]).astype(o_ref.dtype)


def _flash(q, kt, v, bq):
    B, H, S, _ = q.shape
    n_kv, _, bk = kt.shape[2:]
    assert (bq // bk) % 2 == 0 or S == bq
    return pl.pallas_call(
        functools.partial(_flash_kernel, bq=bq, bk=bk),
        out_shape=jax.ShapeDtypeStruct((B, S, H * MLA_V), BF16),
        grid=(B, H, S // bq),
        in_specs=[pl.BlockSpec((1, 1, bq, MLA_QK_PAD), lambda b, h, i: (b, h, i, 0)),
                  pl.BlockSpec((1, 1, n_kv, MLA_QK_PAD, bk), lambda b, h, i: (b, h, 0, 0, 0)),
                  pl.BlockSpec((1, 1, S, MLA_QK_PAD), lambda b, h, i: (b, h, 0, 0))],
        out_specs=pl.BlockSpec((1, bq, MLA_V), lambda b, h, i: (b, i, h)),
        scratch_shapes=[pltpu.VMEM((bq, bk), F32), pltpu.VMEM((bq, bk), F32)],
        compiler_params=_params(3),
        name="mla_flash",
    )(q, kt, v)


def _ffn_body(x, w_in_ref, cw_ref, cb_ref, w_out_ref, g_ref, b_ref, o_ref, halo_ref):
    tm = x.shape[0]

    @pl.when(pl.program_id(1) == 0)
    def _():
        halo_ref[...] = jnp.zeros_like(halo_ref)

    xb = x.astype(BF16)

    def conv(c0, c1):
        u = _dot(xb, w_in_ref[:, c0:c1])
        ext = jnp.concatenate([halo_ref[:, c0:c1], u], axis=0)
        halo_ref[:, c0:c1] = u[tm - SUBLANES:, :]
        s1 = pltpu.roll(ext, 1, 0)[SUBLANES:, :]
        s2 = pltpu.roll(ext, 2, 0)[SUBLANES:, :]
        w = cw_ref[:, c0:c1]
        return w[2:3, :] * u + w[1:2, :] * s1 + w[0:1, :] * s2 + cb_ref[:, c0:c1]

    acc = jnp.zeros((tm, D_MODEL), F32)
    for c0, c1 in FFN_CHUNKS:
        gate = conv(c0, c1)
        val = conv(FFN_DIM + c0, FFN_DIM + c1)
        act = (gate * _sigmoid(gate) * val).astype(BF16)
        acc = acc + _dot(act, w_out_ref[c0:c1, :])
    o_ref[0] = _layernorm(DEEPNORM_ALPHA * x + acc, g_ref[...], b_ref[...])


def _ffn_kernel(x_ref, *rest):
    _ffn_body(x_ref[0], *rest)


def _mix_ffn_kernel(a_ref, x_ref, w_o_ref, g1_ref, b1_ref, *rest):
    h = _dot(a_ref[0], w_o_ref[...])
    _ffn_body(_layernorm(DEEPNORM_ALPHA * x_ref[0] + h, g1_ref[...], b1_ref[...]), *rest)


def _ffn(x, w_in, conv_w, conv_b, w_out, g, b, tm, mixer=None):
    B, S, D = x.shape
    tok = lambda w: pl.BlockSpec((1, tm, w), lambda b_, s: (b_, s, 0))
    args = [x, w_in.astype(BF16), conv_w.astype(F32), _row(conv_b), w_out.astype(BF16), _row(g), _row(b)]
    specs = [tok(D), _const_spec((D, 2 * FFN_DIM)), _const_spec((FFN_CONV_WIDTH, 2 * FFN_DIM)),
             _const_spec((1, 2 * FFN_DIM)), _const_spec((FFN_DIM, D)), _const_spec((1, D)), _const_spec((1, D))]
    body = _ffn_kernel
    if mixer is not None:
        a, w_o, g1, b1 = mixer
        K = a.shape[-1]
        args = [a, x, w_o.astype(BF16), _row(g1), _row(b1)] + args[1:]
        specs = [tok(K), tok(D), _const_spec((K, D)), _const_spec((1, D)), _const_spec((1, D))] + specs[1:]
        body = _mix_ffn_kernel
    return pl.pallas_call(
        body,
        out_shape=jax.ShapeDtypeStruct((B, S, D), F32),
        grid=(B, S // tm),
        in_specs=specs,
        out_specs=tok(D),
        scratch_shapes=[pltpu.VMEM((SUBLANES, 2 * FFN_DIM), F32)],
        compiler_params=_params(2, FFN_VMEM_LIMIT),
        name="conv_ffn_ln",
    )(*args)


def _gla_proj_kernel(x_ref, w_in_ref, w_a2_ref, b_a_ref, q_ref, k_ref, v_ref, la_ref, gate_ref):
    hk = GLA_HEADS * GLA_DK
    hv = GLA_HEADS * GLA_DV
    proj = _dot(x_ref[0].astype(BF16), w_in_ref[...])
    q_ref[0] = (proj[:, :hk] * GLA_DK ** -0.5).astype(BF16)
    k_ref[0] = proj[:, hk:2 * hk].astype(BF16)
    v_ref[0] = proj[:, 2 * hk:2 * hk + hv].astype(BF16)
    r = proj[:, 2 * hk + hv:2 * hk + 2 * hv]
    gate_ref[0] = (r * _sigmoid(r)).astype(BF16)
    a_lr = proj[:, 2 * hk + 2 * hv:].astype(BF16)
    z = _dot(a_lr, w_a2_ref[...]) + b_a_ref[...]
    la_ref[0] = (jnp.minimum(z, 0.0) - jnp.log(1.0 + jnp.exp(-jnp.abs(z)))) * (1.0 / GLA_GATE_NORM)


def _gla_proj(x, w_in, w_a2, b_a, tm):
    B, S, D = x.shape
    hk = GLA_HEADS * GLA_DK
    hv = GLA_HEADS * GLA_DV
    a0 = 2 * hk + hv
    a_cols = jnp.pad(w_in[:, a0:a0 + GLA_GATE_RANK], ((0, 0), (0, LANES - GLA_GATE_RANK)))
    w_ext = jnp.concatenate([w_in[:, :a0], w_in[:, a0 + GLA_GATE_RANK:], a_cols], axis=1).astype(BF16)
    w_a2p = jnp.pad(w_a2, ((0, LANES - GLA_GATE_RANK), (0, 0))).astype(BF16)
    tok = lambda w: pl.BlockSpec((1, tm, w), lambda b_, s: (b_, s, 0))
    sd = lambda w, dt: jax.ShapeDtypeStruct((B, S, w), dt)
    return pl.pallas_call(
        _gla_proj_kernel,
        out_shape=(sd(hk, BF16), sd(hk, BF16), sd(hv, BF16), sd(hk, F32), sd(hv, BF16)),
        grid=(B, S // tm),
        in_specs=[tok(D), _const_spec(w_ext.shape), _const_spec(w_a2p.shape), _const_spec((1, hk))],
        out_specs=(tok(hk), tok(hk), tok(hv), tok(hk), tok(hv)),
        compiler_params=_params(2),
        name="gla_proj",
    )(x, w_ext, w_a2p, _row(b_a))


def _gla_tables(C):
    levels = [1 << e for e in range(int(math.log2(C)))][::-1]
    t = np.arange(C)
    u = np.arange(C)[None, :]
    blocks = []
    for s in levels:
        ref = (t // (2 * s)) * (2 * s) + s - 1
        upper = ((t // s) % 2) == 1
        q_role = (u > ref[:, None]) & (u <= t[:, None])
        k_role = (u > t[:, None]) & (u <= ref[:, None])
        blocks.append(np.where(upper[:, None], q_role, k_role))
    blocks.append(u <= t[:, None])
    sums = np.concatenate(blocks, axis=0).astype(np.float32)
    x = t[:, None] ^ t[None, :]
    lvl = np.full((C, C), -1, np.int32)
    for n, s in enumerate(levels):
        lvl = np.where((t[:, None] > t[None, :]) & (x >= s) & (x < 2 * s), n, lvl)
    lvl = np.where(t[:, None] == t[None, :], len(levels), lvl)
    return len(levels), sums, lvl.astype(np.int32)


def _gla_core_kernel(q_ref, k_ref, v_ref, la_ref, gate_ref, norm_ref, sums_ref, lvl_ref, o_ref, state_ref,
                     *, n_levels):
    C = GLA_CHUNK
    DK, DV = GLA_DK, GLA_DV

    @pl.when(pl.program_id(1) == 0)
    def _():
        state_ref[...] = jnp.zeros_like(state_ref)

    lvl = lvl_ref[...]
    for h in range(GLA_HEADS):
        q = q_ref[0, :, h * DK:(h + 1) * DK].astype(F32)
        k = k_ref[0, :, h * DK:(h + 1) * DK].astype(F32)
        v = v_ref[0, :, h * DV:(h + 1) * DV]
        la = la_ref[0, :, h * DK:(h + 1) * DK]
        hi = la.astype(BF16)
        lo = (la - hi.astype(F32)).astype(BF16)
        e2 = _dot(sums_ref[...], jnp.concatenate([hi, lo], axis=1))
        e = e2[:, :DK] + e2[:, DK:]

        a = jnp.where(lvl == n_levels, jnp.sum(q * k, axis=-1, keepdims=True), 0.0)
        for n in range(n_levels):
            w = jnp.exp(e[n * C:(n + 1) * C, :])
            p = _dot_nt((q * w).astype(BF16), (k * w).astype(BF16))
            a = jnp.where(lvl == n, p, a)
        e_q = e[n_levels * C:(n_levels + 1) * C, :]
        e_k = e_q[C - 1:C, :] - e_q
        st = state_ref[h]
        o = _dot_nt((q * jnp.exp(e_q)).astype(BF16), st.astype(BF16)) + _dot(a.astype(BF16), v)
        k_dec = (k * jnp.exp(e_k)).astype(BF16)
        state_ref[h] = jnp.exp(e_q[C - 1:C, :]) * st + _dot_tn(v, k_dec)
        o = _rmsnorm(o, norm_ref[...])
        o_ref[0, :, h * DV:(h + 1) * DV] = (o * gate_ref[0, :, h * DV:(h + 1) * DV].astype(F32)).astype(BF16)


def _gla_core(q, k, v, la, gate, out_norm):
    B, S, hk = q.shape
    hv = v.shape[-1]
    C = GLA_CHUNK
    n_levels, sums, lvl = _gla_tables(C)
    tok = lambda w: pl.BlockSpec((1, C, w), lambda b_, s: (b_, s, 0))
    return pl.pallas_call(
        functools.partial(_gla_core_kernel, n_levels=n_levels),
        out_shape=jax.ShapeDtypeStruct((B, S, hv), BF16),
        grid=(B, S // C),
        in_specs=[tok(hk), tok(hk), tok(hv), tok(hk), tok(hv), _const_spec((1, GLA_DV)),
                  _const_spec(sums.shape), _const_spec(lvl.shape)],
        out_specs=tok(hv),
        scratch_shapes=[pltpu.VMEM((GLA_HEADS, GLA_DV, GLA_DK), F32)],
        compiler_params=_params(2),
        name="gla_core",
    )(q, k, v, la, gate, _row(out_norm), jnp.asarray(sums, BF16), jnp.asarray(lvl))


def _conformer_kernel(x_ref, w_in_ref, b_in_ref, dw_ref, dwb_ref, lng_ref, lnb_ref, w_o_ref, b_o_ref,
                      g1_ref, b1_ref, o_ref, hist_ref, y_ref, *, tm):
    D = D_MODEL
    HALO = CONV_HALO

    @pl.when(pl.program_id(1) == 0)
    def _():
        hist_ref[0:HALO, :] = jnp.zeros((HALO, D), F32)

    x = x_ref[0]
    h = _dot(x.astype(BF16), w_in_ref[...]) + b_in_ref[...]
    hist_ref[HALO:HALO + tm, :] = h[:, :D] * _sigmoid(h[:, D:])

    n_m = (CONV_WIDTH + SUBLANES - 1) // SUBLANES
    for c0 in range(0, D, LANES):
        y = None
        for r in range(SUBLANES):
            z = None
            for m in range(n_m):
                d = SUBLANES * m + r
                if d >= CONV_WIDTH:
                    continue
                start = HALO - SUBLANES - SUBLANES * m
                tap = CONV_WIDTH - 1 - d
                term = dw_ref[tap:tap + 1, c0:c0 + LANES] * hist_ref[start:start + tm + SUBLANES, c0:c0 + LANES]
                z = term if z is None else z + term
            if r:
                z = pltpu.roll(z, r, 0)
            y = z if y is None else y + z
        y_ref[:, c0:c0 + LANES] = y[SUBLANES:, :] + dwb_ref[:, c0:c0 + LANES]
    hist_ref[0:HALO, :] = hist_ref[tm:tm + HALO, :]

    t = _layernorm(y_ref[...], lng_ref[...], lnb_ref[...])
    t = (t * _sigmoid(t)).astype(BF16)
    out = _dot(t, w_o_ref[...]) + b_o_ref[...]
    o_ref[0] = _layernorm(DEEPNORM_ALPHA * x + out, g1_ref[...], b1_ref[...])


def _conformer(x, w_in, b_in, dw, dw_b, ln_g, ln_b, w_o, b_o, g1, b1, tm):
    B, S, D = x.shape
    return pl.pallas_call(
        functools.partial(_conformer_kernel, tm=tm),
        out_shape=jax.ShapeDtypeStruct((B, S, D), F32),
        grid=(B, S // tm),
        in_specs=[pl.BlockSpec((1, tm, D), lambda b_, s: (b_, s, 0)),
                  _const_spec((D, 2 * D)), _const_spec((1, 2 * D)), _const_spec((CONV_WIDTH, D)),
                  _const_spec((1, D)), _const_spec((1, D)), _const_spec((1, D)), _const_spec((D, D)),
                  _const_spec((1, D)), _const_spec((1, D)), _const_spec((1, D))],
        out_specs=pl.BlockSpec((1, tm, D), lambda b_, s: (b_, s, 0)),
        scratch_shapes=[pltpu.VMEM((CONV_HALO + tm, D), F32), pltpu.VMEM((tm, D), F32)],
        compiler_params=_params(2),
        name="conformer_ln",
    )(x, w_in.astype(BF16), _row(b_in), dw.astype(F32), _row(dw_b), _row(ln_g), _row(ln_b),
      w_o.astype(BF16), _row(b_o), _row(g1), _row(b1))


def _tile(S, want):
    t = min(want, S)
    assert S % t == 0
    return t


def _mla_mix(x, rope, w_in, q_norm, kv_norm, w_uq, w_ukv):
    S = x.shape[1]
    q, kt, v = _mla_proj(x, rope[0], rope[1], w_in, q_norm, kv_norm, w_uq, w_ukv, _tile(S, 2 * TOKEN_TILE))
    return _flash(q, kt, v, _tile(S, MLA_Q_BLOCK))


def _gla_mix(x, w_in, w_a2, b_a, out_norm):
    S = x.shape[1]
    q, k, v, la, gate = _gla_proj(x, w_in, w_a2, b_a, _tile(S, 2 * TOKEN_TILE))
    return _gla_core(q, k, v, la, gate, out_norm)


def kernel(x, positions, l0_mla_w_in, l0_mla_q_norm, l0_mla_kv_norm, l0_mla_w_uq, l0_mla_w_ukv, l0_mla_w_o, l0_ln1_g, l0_ln1_b, l0_ffn_w_in, l0_ffn_conv, l0_ffn_conv_b, l0_ffn_w_out, l0_ln2_g, l0_ln2_b, l1_gla_w_in, l1_gla_w_a2, l1_gla_b_a, l1_gla_out_norm, l1_gla_w_o, l1_ln1_g, l1_ln1_b, l1_ffn_w_in, l1_ffn_conv, l1_ffn_conv_b, l1_ffn_w_out, l1_ln2_g, l1_ln2_b, l2_conv_w_in, l2_conv_b_in, l2_conv_dw, l2_conv_dw_b, l2_conv_ln_g, l2_conv_ln_b, l2_conv_w_o, l2_conv_b_o, l2_ln1_g, l2_ln1_b, l2_ffn_w_in, l2_ffn_conv, l2_ffn_conv_b, l2_ffn_w_out, l2_ln2_g, l2_ln2_b, l3_mla_w_in, l3_mla_q_norm, l3_mla_kv_norm, l3_mla_w_uq, l3_mla_w_ukv, l3_mla_w_o, l3_ln1_g, l3_ln1_b, l3_ffn_w_in, l3_ffn_conv, l3_ffn_conv_b, l3_ffn_w_out, l3_ln2_g, l3_ln2_b):
    S = x.shape[1]
    assert S % GLA_CHUNK == 0 or S < GLA_CHUNK
    tm = _tile(S, FFN_TILE)
    rope = _rope_table(positions, _tile(S, TOKEN_TILE))

    o = _mla_mix(x, rope, l0_mla_w_in, l0_mla_q_norm, l0_mla_kv_norm, l0_mla_w_uq, l0_mla_w_ukv)
    x = _ffn(x, l0_ffn_w_in, l0_ffn_conv, l0_ffn_conv_b, l0_ffn_w_out, l0_ln2_g, l0_ln2_b, tm,
             mixer=(o, l0_mla_w_o, l0_ln1_g, l0_ln1_b))

    o = _gla_mix(x, l1_gla_w_in, l1_gla_w_a2, l1_gla_b_a, l1_gla_out_norm)
    x = _ffn(x, l1_ffn_w_in, l1_ffn_conv, l1_ffn_conv_b, l1_ffn_w_out, l1_ln2_g, l1_ln2_b, tm,
             mixer=(o, l1_gla_w_o, l1_ln1_g, l1_ln1_b))

    x = _conformer(x, l2_conv_w_in, l2_conv_b_in, l2_conv_dw, l2_conv_dw_b, l2_conv_ln_g, l2_conv_ln_b,
                   l2_conv_w_o, l2_conv_b_o, l2_ln1_g, l2_ln1_b, _tile(S, CONFORMER_TILE))
    x = _ffn(x, l2_ffn_w_in, l2_ffn_conv, l2_ffn_conv_b, l2_ffn_w_out, l2_ln2_g, l2_ln2_b, tm)

    o = _mla_mix(x, rope, l3_mla_w_in, l3_mla_q_norm, l3_mla_kv_norm, l3_mla_w_uq, l3_mla_w_ukv)
    x = _ffn(x, l3_ffn_w_in, l3_ffn_conv, l3_ffn_conv_b, l3_ffn_w_out, l3_ln2_g, l3_ln2_b, tm,
             mixer=(o, l3_mla_w_o, l3_ln1_g, l3_ln1_b))
    return x
```

```python
import functools
import math

import numpy as np
import jax
import jax.numpy as jnp
from jax import lax
from jax.experimental import pallas as pl
from jax.experimental.pallas import tpu as pltpu

F32 = jnp.float32
BF16 = jnp.bfloat16

D_MODEL = 1024
DEPTH = 4
MLA_HEADS = 8
MLA_NOPE = 128
MLA_ROPE = 64
MLA_V = 128
MLA_Q_RANK = 256
MLA_KV_RANK = 128
MLA_QK_PAD = 256
MLA_KV_BLOCK = 512
MLA_Q_BLOCK = 2048
FLASH_UNROLL_PAIRS = 2
ROPE_THETA = 10000.0
GLA_HEADS = 4
GLA_DK = 128
GLA_DV = 256
GLA_GATE_RANK = 16
GLA_GATE_NORM = 16.0
GLA_CHUNK = 256
CONV_WIDTH = 31
CONV_HALO = 32
FFN_DIM = 2816
FFN_CONV_WIDTH = 3
FFN_CHUNKS = ((0, 1536), (1536, 2816))
TOKEN_TILE = 512
FFN_TILE = 1024
ROPE_TILE = 2048
CONFORMER_TILE = 1024
NORM_EPS = 1e-5
DEEPNORM_ALPHA = (2.0 * DEPTH) ** 0.25
SUBLANES = 8
LANES = 128
VMEM_LIMIT = 56 * 1024 * 1024
FFN_VMEM_LIMIT = 62 * 1024 * 1024


def _dot(a, b):
    return jnp.dot(a, b, preferred_element_type=F32)


def _dot_nt(a, b):
    return lax.dot_general(a, b, (((1,), (1,)), ((), ())), preferred_element_type=F32)


def _dot_tn(a, b):
    return lax.dot_general(a, b, (((0,), (0,)), ((), ())), preferred_element_type=F32)


def _layernorm(y, g, b):
    mu = jnp.mean(y, axis=-1, keepdims=True)
    d = y - mu
    var = jnp.mean(d * d, axis=-1, keepdims=True)
    return d * lax.rsqrt(var + NORM_EPS) * g + b


def _rmsnorm(y, g):
    return y * lax.rsqrt(jnp.mean(y * y, axis=-1, keepdims=True) + NORM_EPS) * g


def _sigmoid(x):
    return 1.0 / (1.0 + jnp.exp(-x))


def _const_spec(shape):
    nd = len(shape)
    return pl.BlockSpec(shape, lambda *_: (0,) * nd, pipeline_mode=pl.Buffered(1))


def _params(n_axes, vmem_limit=VMEM_LIMIT):
    return pltpu.CompilerParams(dimension_semantics=("arbitrary",) * n_axes, vmem_limit_bytes=vmem_limit)


def _row(v):
    return v.reshape(1, -1).astype(F32)


def _rope_table_kernel(pos_ref, freq_ref, cos_ref, sin_ref):
    ang = pos_ref[0] * freq_ref[...]
    cos_ref[0] = jnp.cos(ang)
    sin_ref[0] = jnp.sin(ang)


def _rope_table(positions, tm):
    B, S = positions.shape
    half = MLA_ROPE // 2
    inv_freq = ROPE_THETA ** (-jnp.arange(half, dtype=F32) / half)
    freq = jnp.tile(inv_freq, LANES // half).reshape(1, LANES)
    posf = positions.astype(F32).reshape(B, S, 1)
    out = jax.ShapeDtypeStruct((B, S, LANES), F32)
    return pl.pallas_call(
        _rope_table_kernel,
        out_shape=(out, out),
        grid=(B, S // tm),
        in_specs=[pl.BlockSpec((1, tm, 1), lambda b, s: (b, s, 0)), _const_spec((1, LANES))],
        out_specs=(pl.BlockSpec((1, tm, LANES), lambda b, s: (b, s, 0)),
                   pl.BlockSpec((1, tm, LANES), lambda b, s: (b, s, 0))),
        compiler_params=_params(2),
        name="rope_table",
    )(posf, freq)


def _mla_proj_kernel(x_ref, cos_ref, sin_ref, w_in_ref, qn_ref, kvn_ref, w_uq_ref, w_ukt_ref, w_uv_ref,
                     q_ref, kt_ref, v_ref):
    H = MLA_HEADS
    tm = x_ref.shape[1]
    bk = kt_ref.shape[-1]
    x = x_ref[0].astype(BF16)
    c = _dot(x, w_in_ref[...])
    cos = cos_ref[0]
    sin = sin_ref[0]
    c_q = _rmsnorm(c[:, :MLA_Q_RANK], qn_ref[...]).astype(BF16)
    c_kv = _rmsnorm(c[:, MLA_Q_RANK:MLA_Q_RANK + MLA_KV_RANK], kvn_ref[...])
    o = MLA_Q_RANK + MLA_KV_RANK
    k_rope_t = (c[:, o:o + LANES] * cos + c[:, o + LANES:o + 2 * LANES] * sin).T.astype(BF16)
    scale = (MLA_NOPE + MLA_ROPE) ** -0.5 * math.log2(math.e)
    q_all = _dot(c_q, w_uq_ref[...]) * scale
    k_nope_t = _dot(w_ukt_ref[...], c_kv.T.astype(BF16))
    v = _dot(c_kv.astype(BF16), w_uv_ref[...])
    lane = lax.broadcasted_iota(jnp.int32, (tm, LANES), 1)
    ones_col = jnp.where(lane == 0, 1.0, 0.0).astype(BF16)
    for h in range(H):
        q_ref[0, h, :, 0:LANES] = q_all[:, h * LANES:(h + 1) * LANES].astype(BF16)
        qr = (q_all[:, (H + h) * LANES:(H + h + 1) * LANES] * cos
              + q_all[:, (2 * H + h) * LANES:(2 * H + h + 1) * LANES] * sin)
        q_ref[0, h, :, LANES:2 * LANES] = qr.astype(BF16)
        for blk in range(tm // bk):
            cols = slice(blk * bk, (blk + 1) * bk)
            kt_ref[0, h, blk, 0:LANES, :] = k_nope_t[h * LANES:(h + 1) * LANES, cols].astype(BF16)
            kt_ref[0, h, blk, LANES:2 * LANES, :] = k_rope_t[:, cols]
        v_ref[0, h, :, 0:LANES] = v[:, h * LANES:(h + 1) * LANES].astype(BF16)
        v_ref[0, h, :, LANES:2 * LANES] = ones_col


def _rot_half_cols(w):
    half = w.shape[1] // 2
    return jnp.concatenate([-w[:, half:], w[:, :half]], axis=1)


def _mla_proj(x, cos, sin, w_in, q_norm, kv_norm, w_uq, w_ukv, tm):
    B, S, D = x.shape
    H = MLA_HEADS
    o = MLA_Q_RANK + MLA_KV_RANK
    kr = w_in[:, o:]
    krot = _rot_half_cols(kr)
    w_in_ext = jnp.concatenate([w_in[:, :o], kr, kr, krot, krot], axis=1).astype(BF16)
    w_uq3 = w_uq.reshape(MLA_Q_RANK, H, MLA_NOPE + MLA_ROPE)
    nope = w_uq3[:, :, :MLA_NOPE].reshape(MLA_Q_RANK, H * MLA_NOPE)
    rope = w_uq3[:, :, MLA_NOPE:]
    rot = jnp.concatenate([-rope[:, :, MLA_ROPE // 2:], rope[:, :, :MLA_ROPE // 2]], axis=2)
    zpad = jnp.zeros((MLA_Q_RANK, H, LANES - MLA_ROPE), w_uq.dtype)
    rope_p = jnp.concatenate([rope, zpad], axis=2).reshape(MLA_Q_RANK, H * LANES)
    rot_p = jnp.concatenate([rot, zpad], axis=2).reshape(MLA_Q_RANK, H * LANES)
    w_uq_ext = jnp.concatenate([nope, rope_p, rot_p], axis=1).astype(BF16)
    w_ukv3 = w_ukv.reshape(MLA_KV_RANK, H, MLA_NOPE + MLA_V)
    w_ukt = w_ukv3[:, :, :MLA_NOPE].reshape(MLA_KV_RANK, H * MLA_NOPE).T.astype(BF16)
    w_uv = w_ukv3[:, :, MLA_NOPE:].reshape(MLA_KV_RANK, H * MLA_V).astype(BF16)
    qv = jax.ShapeDtypeStruct((B, H, S, MLA_QK_PAD), BF16)
    bk = min(tm, MLA_KV_BLOCK)
    kt = jax.ShapeDtypeStruct((B, H, S // bk, MLA_QK_PAD, bk), BF16)
    tok = lambda w: pl.BlockSpec((1, tm, w), lambda b, s: (b, s, 0))
    hd = pl.BlockSpec((1, H, tm, MLA_QK_PAD), lambda b, s: (b, 0, s, 0))
    return pl.pallas_call(
        _mla_proj_kernel,
        out_shape=(qv, kt, qv),
        grid=(B, S // tm),
        in_specs=[tok(D), tok(LANES), tok(LANES), _const_spec(w_in_ext.shape), _const_spec((1, MLA_Q_RANK)),
                  _const_spec((1, MLA_KV_RANK)), _const_spec(w_uq_ext.shape), _const_spec(w_ukt.shape),
                  _const_spec(w_uv.shape)],
        out_specs=(hd, pl.BlockSpec((1, H, tm // bk, MLA_QK_PAD, bk), lambda b, s: (b, 0, s, 0, 0)), hd),
        compiler_params=_params(2),
        name="mla_proj",
    )(x, cos, sin, w_in_ext, _row(q_norm), _row(kv_norm), w_uq_ext, w_ukt, w_uv)


def _flash_kernel(q_ref, kt_ref, v_ref, o_ref, sa_ref, sb_ref, *, bq, bk):
    qi = pl.program_id(2)
    n_sub = bq // bk
    q = q_ref[0, 0]

    def scores(q_rows, j):
        return _dot(q_rows, kt_ref[0, 0, j])

    def update(s, j, m, acc, masked):
        start = pl.multiple_of(j * bk, bk)
        v = v_ref[0, 0, pl.ds(start, bk), :]
        if masked:
            row = lax.broadcasted_iota(jnp.int32, s.shape, 0)
            col = lax.broadcasted_iota(jnp.int32, s.shape, 1)
            s = jnp.where(col <= row, s, -jnp.inf)
        m_new = jnp.maximum(m, jnp.max(s, axis=-1, keepdims=True))
        p = jnp.exp2(s - m_new)
        acc = jnp.exp2(m - m_new) * acc + _dot(p.astype(BF16), v)
        return m_new, acc

    def pairs(n_pairs, base):
        def body(t, carry):
            m, acc = carry
            j = base + 2 * n_pairs * t
            for u in range(n_pairs):
                sb_ref[...] = scores(q, j + 2 * u + 1)
                m, acc = update(sa_ref[...], j + 2 * u, m, acc, False)
                sa_ref[...] = scores(q, j + 2 * u + 2)
                m, acc = update(sb_ref[...], j + 2 * u + 1, m, acc, False)
            return m, acc
        return body

    n_off = n_sub * qi
    m = jnp.full((bq, 1), -jnp.inf, F32)
    acc = jnp.zeros((bq, 2 * MLA_V), F32)
    sa_ref[...] = scores(q, 0)
    n_long = n_off // (2 * FLASH_UNROLL_PAIRS)
    m, acc = lax.fori_loop(0, n_long, pairs(FLASH_UNROLL_PAIRS, 0), (m, acc))
    if n_sub % (2 * FLASH_UNROLL_PAIRS):
        done = 2 * FLASH_UNROLL_PAIRS * n_long
        m, acc = lax.fori_loop(0, (n_off - done) // 2, pairs(1, done), (m, acc))
    s = sa_ref[...]
    for t in range(n_sub):
        r0 = t * bk
        s_next = scores(q[r0 + bk:], n_off + t + 1) if t + 1 < n_sub else None
        m_t, acc_t = update(s, n_off + t, m[r0:], acc[r0:], True)
        m = m_t if t == 0 else jnp.concatenate([m[:r0], m_t], axis=0)
        acc = acc_t if t == 0 else jnp.concatenate([acc[:r0], acc_t], axis=0)
        s = s_next
    o_ref[0] = (acc[:, :MLA_V] / acc[:, MLA_V:MLA_V + 1]).astype(o_ref.dtype)


def _flash(q, kt, v, bq):
    B, H, S, _ = q.shape
    n_kv, _, bk = kt.shape[2:]
    assert (bq // bk) % 2 == 0 or S == bq
    return pl.pallas_call(
        functools.partial(_flash_kernel, bq=bq, bk=bk),
        out_shape=jax.ShapeDtypeStruct((B, S, H * MLA_V), BF16),
        grid=(B, H, S // bq),
        in_specs=[pl.BlockSpec((1, 1, bq, MLA_QK_PAD), lambda b, h, i: (b, h, i, 0)),
                  pl.BlockSpec((1, 1, n_kv, MLA_QK_PAD, bk), lambda b, h, i: (b, h, 0, 0, 0)),
                  pl.BlockSpec((1, 1, S, MLA_QK_PAD), lambda b, h, i: (b, h, 0, 0))],
        out_specs=pl.BlockSpec((1, bq, MLA_V), lambda b, h, i: (b, i, h)),
        scratch_shapes=[pltpu.VMEM((bq, bk), F32), pltpu.VMEM((bq, bk), F32)],
        compiler_params=_params(3),
        name="mla_flash",
    )(q, kt, v)


def _ffn_body(x, w_in_ref, cw_ref, cb_ref, w_out_ref, g_ref, b_ref, o_ref, halo_ref):
    tm = x.shape[0]

    @pl.when(pl.program_id(1) == 0)
    def _():
        halo_ref[...] = jnp.zeros_like(halo_ref)

    xb = x.astype(BF16)

    def conv(c0, c1):
        u = _dot(xb, w_in_ref[:, c0:c1])
        ext = jnp.concatenate([halo_ref[:, c0:c1], u], axis=0)
        halo_ref[:, c0:c1] = u[tm - SUBLANES:, :]
        s1 = pltpu.roll(ext, 1, 0)[SUBLANES:, :]
        s2 = pltpu.roll(ext, 2, 0)[SUBLANES:, :]
        w = cw_ref[:, c0:c1]
        return w[2:3, :] * u + w[1:2, :] * s1 + w[0:1, :] * s2 + cb_ref[:, c0:c1]

    acc = jnp.zeros((tm, D_MODEL), F32)
    for c0, c1 in FFN_CHUNKS:
        gate = conv(c0, c1)
        val = conv(FFN_DIM + c0, FFN_DIM + c1)
        act = (gate * _sigmoid(gate) * val).astype(BF16)
        acc = acc + _dot(act, w_out_ref[c0:c1, :])
    o_ref[0] = _layernorm(DEEPNORM_ALPHA * x + acc, g_ref[...], b_ref[...])


def _ffn_kernel(x_ref, *rest):
    _ffn_body(x_ref[0], *rest)


def _mix_ffn_kernel(a_ref, x_ref, w_o_ref, g1_ref, b1_ref, *rest):
    h = _dot(a_ref[0], w_o_ref[...])
    _ffn_body(_layernorm(DEEPNORM_ALPHA * x_ref[0] + h, g1_ref[...], b1_ref[...]), *rest)


def _ffn(x, w_in, conv_w, conv_b, w_out, g, b, tm, mixer=None):
    B, S, D = x.shape
    tok = lambda w: pl.BlockSpec((1, tm, w), lambda b_, s: (b_, s, 0))
    args = [x, w_in.astype(BF16), conv_w.astype(F32), _row(conv_b), w_out.astype(BF16), _row(g), _row(b)]
    specs = [tok(D), _const_spec((D, 2 * FFN_DIM)), _const_spec((FFN_CONV_WIDTH, 2 * FFN_DIM)),
             _const_spec((1, 2 * FFN_DIM)), _const_spec((FFN_DIM, D)), _const_spec((1, D)), _const_spec((1, D))]
    body = _ffn_kernel
    if mixer is not None:
        a, w_o, g1, b1 = mixer
        K = a.shape[-1]
        args = [a, x, w_o.astype(BF16), _row(g1), _row(b1)] + args[1:]
        specs = [tok(K), tok(D), _const_spec((K, D)), _const_spec((1, D)), _const_spec((1, D))] + specs[1:]
        body = _mix_ffn_kernel
    return pl.pallas_call(
        body,
        out_shape=jax.ShapeDtypeStruct((B, S, D), F32),
        grid=(B, S // tm),
        in_specs=specs,
        out_specs=tok(D),
        scratch_shapes=[pltpu.VMEM((SUBLANES, 2 * FFN_DIM), F32)],
        compiler_params=_params(2, FFN_VMEM_LIMIT),
        name="conv_ffn_ln",
    )(*args)


def _gla_proj_kernel(x_ref, w_in_ref, w_a2_ref, b_a_ref, q_ref, k_ref, v_ref, la_ref, gate_ref):
    hk = GLA_HEADS * GLA_DK
    hv = GLA_HEADS * GLA_DV
    proj = _dot(x_ref[0].astype(BF16), w_in_ref[...])
    q_ref[0] = (proj[:, :hk] * GLA_DK ** -0.5).astype(BF16)
    k_ref[0] = proj[:, hk:2 * hk].astype(BF16)
    v_ref[0] = proj[:, 2 * hk:2 * hk + hv].astype(BF16)
    r = proj[:, 2 * hk + hv:2 * hk + 2 * hv]
    gate_ref[0] = (r * _sigmoid(r)).astype(BF16)
    a_lr = proj[:, 2 * hk + 2 * hv:].astype(BF16)
    z = _dot(a_lr, w_a2_ref[...]) + b_a_ref[...]
    la_ref[0] = (jnp.minimum(z, 0.0) - jnp.log(1.0 + jnp.exp(-jnp.abs(z)))) * (1.0 / GLA_GATE_NORM)


def _gla_proj(x, w_in, w_a2, b_a, tm):
    B, S, D = x.shape
    hk = GLA_HEADS * GLA_DK
    hv = GLA_HEADS * GLA_DV
    a0 = 2 * hk + hv
    a_cols = jnp.pad(w_in[:, a0:a0 + GLA_GATE_RANK], ((0, 0), (0, LANES - GLA_GATE_RANK)))
    w_ext = jnp.concatenate([w_in[:, :a0], w_in[:, a0 + GLA_GATE_RANK:], a_cols], axis=1).astype(BF16)
    w_a2p = jnp.pad(w_a2, ((0, LANES - GLA_GATE_RANK), (0, 0))).astype(BF16)
    tok = lambda w: pl.BlockSpec((1, tm, w), lambda b_, s: (b_, s, 0))
    sd = lambda w, dt: jax.ShapeDtypeStruct((B, S, w), dt)
    return pl.pallas_call(
        _gla_proj_kernel,
        out_shape=(sd(hk, BF16), sd(hk, BF16), sd(hv, BF16), sd(hk, F32), sd(hv, BF16)),
        grid=(B, S // tm),
        in_specs=[tok(D), _const_spec(w_ext.shape), _const_spec(w_a2p.shape), _const_spec((1, hk))],
        out_specs=(tok(hk), tok(hk), tok(hv), tok(hk), tok(hv)),
        compiler_params=_params(2),
        name="gla_proj",
    )(x, w_ext, w_a2p, _row(b_a))


def _gla_tables(C):
    levels = [1 << e for e in range(int(math.log2(C)))][::-1]
    t = np.arange(C)
    u = np.arange(C)[None, :]
    blocks = []
    for s in levels:
        ref = (t // (2 * s)) * (2 * s) + s - 1
        upper = ((t // s) % 2) == 1
        q_role = (u > ref[:, None]) & (u <= t[:, None])
        k_role = (u > t[:, None]) & (u <= ref[:, None])
        blocks.append(np.where(upper[:, None], q_role, k_role))
    blocks.append(u <= t[:, None])
    sums = np.concatenate(blocks, axis=0).astype(np.float32)
    x = t[:, None] ^ t[None, :]
    lvl = np.full((C, C), -1, np.int32)
    for n, s in enumerate(levels):
        lvl = np.where((t[:, None] > t[None, :]) & (x >= s) & (x < 2 * s), n, lvl)
    lvl = np.where(t[:, None] == t[None, :], len(levels), lvl)
    return len(levels), sums, lvl.astype(np.int32)


def _gla_core_kernel(q_ref, k_ref, v_ref, la_ref, gate_ref, norm_ref, sums_ref, lvl_ref, o_ref, state_ref,
                     *, n_levels):
    C = GLA_CHUNK
    DK, DV = GLA_DK, GLA_DV

    @pl.when(pl.program_id(1) == 0)
    def _():
        state_ref[...] = jnp.zeros_like(state_ref)

    lvl = lvl_ref[...]
    for h in range(GLA_HEADS):
        q = q_ref[0, :, h * DK:(h + 1) * DK].astype(F32)
        k = k_ref[0, :, h * DK:(h + 1) * DK].astype(F32)
        v = v_ref[0, :, h * DV:(h + 1) * DV]
        la = la_ref[0, :, h * DK:(h + 1) * DK]
        hi = la.astype(BF16)
        lo = (la - hi.astype(F32)).astype(BF16)
        e2 = _dot(sums_ref[...], jnp.concatenate([hi, lo], axis=1))
        e = e2[:, :DK] + e2[:, DK:]

        a = jnp.where(lvl == n_levels, jnp.sum(q * k, axis=-1, keepdims=True), 0.0)
        for n in range(n_levels):
            w = jnp.exp(e[n * C:(n + 1) * C, :])
            p = _dot_nt((q * w).astype(BF16), (k * w).astype(BF16))
            a = jnp.where(lvl == n, p, a)
        e_q = e[n_levels * C:(n_levels + 1) * C, :]
        e_k = e_q[C - 1:C, :] - e_q
        st = state_ref[h]
        o = _dot_nt((q * jnp.exp(e_q)).astype(BF16), st.astype(BF16)) + _dot(a.astype(BF16), v)
        k_dec = (k * jnp.exp(e_k)).astype(BF16)
        state_ref[h] = jnp.exp(e_q[C - 1:C, :]) * st + _dot_tn(v, k_dec)
        o = _rmsnorm(o, norm_ref[...])
        o_ref[0, :, h * DV:(h + 1) * DV] = (o * gate_ref[0, :, h * DV:(h + 1) * DV].astype(F32)).astype(BF16)


def _gla_core(q, k, v, la, gate, out_norm):
    B, S, hk = q.shape
    hv = v.shape[-1]
    C = GLA_CHUNK
    n_levels, sums, lvl = _gla_tables(C)
    tok = lambda w: pl.BlockSpec((1, C, w), lambda b_, s: (b_, s, 0))
    return pl.pallas_call(
        functools.partial(_gla_core_kernel, n_levels=n_levels),
        out_shape=jax.ShapeDtypeStruct((B, S, hv), BF16),
        grid=(B, S // C),
        in_specs=[tok(hk), tok(hk), tok(hv), tok(hk), tok(hv), _const_spec((1, GLA_DV)),
                  _const_spec(sums.shape), _const_spec(lvl.shape)],
        out_specs=tok(hv),
        scratch_shapes=[pltpu.VMEM((GLA_HEADS, GLA_DV, GLA_DK), F32)],
        compiler_params=_params(2),
        name="gla_core",
    )(q, k, v, la, gate, _row(out_norm), jnp.asarray(sums, BF16), jnp.asarray(lvl))


def _conformer_kernel(x_ref, w_in_ref, b_in_ref, dw_ref, dwb_ref, lng_ref, lnb_ref, w_o_ref, b_o_ref,
                      g1_ref, b1_ref, o_ref, hist_ref, y_ref, *, tm):
    D = D_MODEL
    HALO = CONV_HALO

    @pl.when(pl.program_id(1) == 0)
    def _():
        hist_ref[0:HALO, :] = jnp.zeros((HALO, D), F32)

    x = x_ref[0]
    h = _dot(x.astype(BF16), w_in_ref[...]) + b_in_ref[...]
    hist_ref[HALO:HALO + tm, :] = h[:, :D] * _sigmoid(h[:, D:])

    n_m = (CONV_WIDTH + SUBLANES - 1) // SUBLANES
    for c0 in range(0, D, LANES):
        y = None
        for r in range(SUBLANES):
            z = None
            for m in range(n_m):
                d = SUBLANES * m + r
                if d >= CONV_WIDTH:
                    continue
                start = HALO - SUBLANES - SUBLANES * m
                tap = CONV_WIDTH - 1 - d
                term = dw_ref[tap:tap + 1, c0:c0 + LANES] * hist_ref[start:start + tm + SUBLANES, c0:c0 + LANES]
                z = term if z is None else z + term
            if r:
                z = pltpu.roll(z, r, 0)
            y = z if y is None else y + z
        y_ref[:, c0:c0 + LANES] = y[SUBLANES:, :] + dwb_ref[:, c0:c0 + LANES]
    hist_ref[0:HALO, :] = hist_ref[tm:tm + HALO, :]

    t = _layernorm(y_ref[...], lng_ref[...], lnb_ref[...])
    t = (t * _sigmoid(t)).astype(BF16)
    out = _dot(t, w_o_ref[...]) + b_o_ref[...]
    o_ref[0] = _layernorm(DEEPNORM_ALPHA * x + out, g1_ref[...], b1_ref[...])


def _conformer(x, w_in, b_in, dw, dw_b, ln_g, ln_b, w_o, b_o, g1, b1, tm):
    B, S, D = x.shape
    return pl.pallas_call(
        functools.partial(_conformer_kernel, tm=tm),
        out_shape=jax.ShapeDtypeStruct((B, S, D), F32),
        grid=(B, S // tm),
        in_specs=[pl.BlockSpec((1, tm, D), lambda b_, s: (b_, s, 0)),
                  _const_spec((D, 2 * D)), _const_spec((1, 2 * D)), _const_spec((CONV_WIDTH, D)),
                  _const_spec((1, D)), _const_spec((1, D)), _const_spec((1, D)), _const_spec((D, D)),
                  _const_spec((1, D)), _const_spec((1, D)), _const_spec((1, D))],
        out_specs=pl.BlockSpec((1, tm, D), lambda b_, s: (b_, s, 0)),
        scratch_shapes=[pltpu.VMEM((CONV_HALO + tm, D), F32), pltpu.VMEM((tm, D), F32)],
        compiler_params=_params(2),
        name="conformer_ln",
    )(x, w_in.astype(BF16), _row(b_in), dw.astype(F32), _row(dw_b), _row(ln_g), _row(ln_b),
      w_o.astype(BF16), _row(b_o), _row(g1), _row(b1))


def _tile(S, want):
    t = min(want, S)
    assert S % t == 0
    return t


def _mla_mix(x, rope, w_in, q_norm, kv_norm, w_uq, w_ukv):
    S = x.shape[1]
    q, kt, v = _mla_proj(x, rope[0], rope[1], w_in, q_norm, kv_norm, w_uq, w_ukv, _tile(S, 2 * TOKEN_TILE))
    return _flash(q, kt, v, _tile(S, MLA_Q_BLOCK))


def _gla_mix(x, w_in, w_a2, b_a, out_norm):
    S = x.shape[1]
    q, k, v, la, gate = _gla_proj(x, w_in, w_a2, b_a, _tile(S, 2 * TOKEN_TILE))
    return _gla_core(q, k, v, la, gate, out_norm)


def kernel(x, positions, l0_mla_w_in, l0_mla_q_norm, l0_mla_kv_norm, l0_mla_w_uq, l0_mla_w_ukv, l0_mla_w_o, l0_ln1_g, l0_ln1_b, l0_ffn_w_in, l0_ffn_conv, l0_ffn_conv_b, l0_ffn_w_out, l0_ln2_g, l0_ln2_b, l1_gla_w_in, l1_gla_w_a2, l1_gla_b_a, l1_gla_out_norm, l1_gla_w_o, l1_ln1_g, l1_ln1_b, l1_ffn_w_in, l1_ffn_conv, l1_ffn_conv_b, l1_ffn_w_out, l1_ln2_g, l1_ln2_b, l2_conv_w_in, l2_conv_b_in, l2_conv_dw, l2_conv_dw_b, l2_conv_ln_g, l2_conv_ln_b, l2_conv_w_o, l2_conv_b_o, l2_ln1_g, l2_ln1_b, l2_ffn_w_in, l2_ffn_conv, l2_ffn_conv_b, l2_ffn_w_out, l2_ln2_g, l2_ln2_b, l3_mla_w_in, l3_mla_q_norm, l3_mla_kv_norm, l3_mla_w_uq, l3_mla_w_ukv, l3_mla_w_o, l3_ln1_g, l3_ln1_b, l3_ffn_w_in, l3_ffn_conv, l3_ffn_conv_b, l3_ffn_w_out, l3_ln2_g, l3_ln2_b):
    S = x.shape[1]
    assert S % GLA_CHUNK == 0 or S < GLA_CHUNK
    tm = _tile(S, FFN_TILE)
    rope = _rope_table(positions, _tile(S, ROPE_TILE))

    o = _mla_mix(x, rope, l0_mla_w_in, l0_mla_q_norm, l0_mla_kv_norm, l0_mla_w_uq, l0_mla_w_ukv)
    x = _ffn(x, l0_ffn_w_in, l0_ffn_conv, l0_ffn_conv_b, l0_ffn_w_out, l0_ln2_g, l0_ln2_b, tm,
             mixer=(o, l0_mla_w_o, l0_ln1_g, l0_ln1_b))

    o = _gla_mix(x, l1_gla_w_in, l1_gla_w_a2, l1_gla_b_a, l1_gla_out_norm)
    x = _ffn(x, l1_ffn_w_in, l1_ffn_conv, l1_ffn_conv_b, l1_ffn_w_out, l1_ln2_g, l1_ln2_b, tm,
             mixer=(o, l1_gla_w_o, l1_ln1_g, l1_ln1_b))

    x = _conformer(x, l2_conv_w_in, l2_conv_b_in, l2_conv_dw, l2_conv_dw_b, l2_conv_ln_g, l2_conv_ln_b,
                   l2_conv_w_o, l2_conv_b_o, l2_ln1_g, l2_ln1_b, _tile(S, CONFORMER_TILE))
    x = _ffn(x, l2_ffn_w_in, l2_ffn_conv, l2_ffn_conv_b, l2_ffn_w_out, l2_ln2_g, l2_ln2_b, tm)

    o = _mla_mix(x, rope, l3_mla_w_in, l3_mla_q_norm, l3_mla_kv_norm, l3_mla_w_uq, l3_mla_w_ukv)
    x = _ffn(x, l3_ffn_w_in, l3_ffn_conv, l3_ffn_conv_b, l3_ffn_w_out, l3_ln2_g, l3_ln2_b, tm,
             mixer=(o, l3_mla_w_o, l3_ln1_g, l3_ln1_b))
    return x
```

```python
import functools
import math

import numpy as np
import jax
import jax.numpy as jnp
from jax import lax
from jax.experimental import pallas as pl
from jax.experimental.pallas import tpu as pltpu

F32 = jnp.float32
BF16 = jnp.bfloat16

D_MODEL = 1024
DEPTH = 4
MLA_HEADS = 8
MLA_NOPE = 128
MLA_ROPE = 64
MLA_V = 128
MLA_Q_RANK = 256
MLA_KV_RANK = 128
MLA_QK_PAD = 256
MLA_KV_BLOCK = 512
MLA_Q_BLOCK = 2048
FLASH_UNROLL_PAIRS = 2
ROPE_THETA = 10000.0
GLA_HEADS = 4
GLA_DK = 128
GLA_DV = 256
GLA_GATE_RANK = 16
GLA_GATE_NORM = 16.0
GLA_CHUNK = 256
CONV_WIDTH = 31
CONV_HALO = 32
FFN_DIM = 2816
FFN_CONV_WIDTH = 3
FFN_CHUNKS = ((0, 1536), (1536, 2816))
TOKEN_TILE = 512
FFN_TILE = 1024
CONFORMER_TILE = 512
NORM_EPS = 1e-5
DEEPNORM_ALPHA = (2.0 * DEPTH) ** 0.25
SUBLANES = 8
LANES = 128
VMEM_LIMIT = 56 * 1024 * 1024
FFN_VMEM_LIMIT = 62 * 1024 * 1024


def _dot(a, b):
    return jnp.dot(a, b, preferred_element_type=F32)


def _dot_nt(a, b):
    return lax.dot_general(a, b, (((1,), (1,)), ((), ())), preferred_element_type=F32)


def _dot_tn(a, b):
    return lax.dot_general(a, b, (((0,), (0,)), ((), ())), preferred_element_type=F32)


def _layernorm(y, g, b):
    mu = jnp.mean(y, axis=-1, keepdims=True)
    d = y - mu
    var = jnp.mean(d * d, axis=-1, keepdims=True)
    return d * lax.rsqrt(var + NORM_EPS) * g + b


def _rmsnorm(y, g):
    return y * lax.rsqrt(jnp.mean(y * y, axis=-1, keepdims=True) + NORM_EPS) * g


def _sigmoid(x):
    return 1.0 / (1.0 + jnp.exp(-x))


def _const_spec(shape):
    nd = len(shape)
    return pl.BlockSpec(shape, lambda *_: (0,) * nd, pipeline_mode=pl.Buffered(1))


def _params(n_axes, vmem_limit=VMEM_LIMIT):
    return pltpu.CompilerParams(dimension_semantics=("arbitrary",) * n_axes, vmem_limit_bytes=vmem_limit)


def _row(v):
    return v.reshape(1, -1).astype(F32)


def _rope_table_kernel(pos_ref, freq_ref, cos_ref, sin_ref):
    ang = pos_ref[0] * freq_ref[...]
    cos_ref[0] = jnp.cos(ang)
    sin_ref[0] = jnp.sin(ang)


def _rope_table(positions, tm):
    B, S = positions.shape
    half = MLA_ROPE // 2
    inv_freq = ROPE_THETA ** (-jnp.arange(half, dtype=F32) / half)
    freq = jnp.tile(inv_freq, LANES // half).reshape(1, LANES)
    posf = positions.astype(F32).reshape(B, S, 1)
    out = jax.ShapeDtypeStruct((B, S, LANES), F32)
    return pl.pallas_call(
        _rope_table_kernel,
        out_shape=(out, out),
        grid=(B, S // tm),
        in_specs=[pl.BlockSpec((1, tm, 1), lambda b, s: (b, s, 0)), _const_spec((1, LANES))],
        out_specs=(pl.BlockSpec((1, tm, LANES), lambda b, s: (b, s, 0)),
                   pl.BlockSpec((1, tm, LANES), lambda b, s: (b, s, 0))),
        compiler_params=_params(2),
        name="rope_table",
    )(posf, freq)


def _mla_proj_kernel(x_ref, cos_ref, sin_ref, w_in_ref, qn_ref, kvn_ref, w_uq_ref, w_ukt_ref, w_uv_ref,
                     q_ref, kt_ref, v_ref):
    H = MLA_HEADS
    tm = x_ref.shape[1]
    bk = kt_ref.shape[-1]
    x = x_ref[0].astype(BF16)
    c = _dot(x, w_in_ref[...])
    cos = cos_ref[0]
    sin = sin_ref[0]
    c_q = _rmsnorm(c[:, :MLA_Q_RANK], qn_ref[...]).astype(BF16)
    c_kv = _rmsnorm(c[:, MLA_Q_RANK:MLA_Q_RANK + MLA_KV_RANK], kvn_ref[...])
    o = MLA_Q_RANK + MLA_KV_RANK
    k_rope_t = (c[:, o:o + LANES] * cos + c[:, o + LANES:o + 2 * LANES] * sin).T.astype(BF16)
    scale = (MLA_NOPE + MLA_ROPE) ** -0.5 * math.log2(math.e)
    q_all = _dot(c_q, w_uq_ref[...]) * scale
    k_nope_t = _dot(w_ukt_ref[...], c_kv.T.astype(BF16))
    v = _dot(c_kv.astype(BF16), w_uv_ref[...])
    lane = lax.broadcasted_iota(jnp.int32, (tm, LANES), 1)
    ones_col = jnp.where(lane == 0, 1.0, 0.0).astype(BF16)
    for h in range(H):
        q_ref[0, h, :, 0:LANES] = q_all[:, h * LANES:(h + 1) * LANES].astype(BF16)
        qr = (q_all[:, (H + h) * LANES:(H + h + 1) * LANES] * cos
              + q_all[:, (2 * H + h) * LANES:(2 * H + h + 1) * LANES] * sin)
        q_ref[0, h, :, LANES:2 * LANES] = qr.astype(BF16)
        for blk in range(tm // bk):
            cols = slice(blk * bk, (blk + 1) * bk)
            kt_ref[0, h, blk, 0:LANES, :] = k_nope_t[h * LANES:(h + 1) * LANES, cols].astype(BF16)
            kt_ref[0, h, blk, LANES:2 * LANES, :] = k_rope_t[:, cols]
        v_ref[0, h, :, 0:LANES] = v[:, h * LANES:(h + 1) * LANES].astype(BF16)
        v_ref[0, h, :, LANES:2 * LANES] = ones_col


def _rot_half_cols(w):
    half = w.shape[1] // 2
    return jnp.concatenate([-w[:, half:], w[:, :half]], axis=1)


def _mla_proj(x, cos, sin, w_in, q_norm, kv_norm, w_uq, w_ukv, tm):
    B, S, D = x.shape
    H = MLA_HEADS
    o = MLA_Q_RANK + MLA_KV_RANK
    kr = w_in[:, o:]
    krot = _rot_half_cols(kr)
    w_in_ext = jnp.concatenate([w_in[:, :o], kr, kr, krot, krot], axis=1).astype(BF16)
    w_uq3 = w_uq.reshape(MLA_Q_RANK, H, MLA_NOPE + MLA_ROPE)
    nope = w_uq3[:, :, :MLA_NOPE].reshape(MLA_Q_RANK, H * MLA_NOPE)
    rope = w_uq3[:, :, MLA_NOPE:]
    rot = jnp.concatenate([-rope[:, :, MLA_ROPE // 2:], rope[:, :, :MLA_ROPE // 2]], axis=2)
    zpad = jnp.zeros((MLA_Q_RANK, H, LANES - MLA_ROPE), w_uq.dtype)
    rope_p = jnp.concatenate([rope, zpad], axis=2).reshape(MLA_Q_RANK, H * LANES)
    rot_p = jnp.concatenate([rot, zpad], axis=2).reshape(MLA_Q_RANK, H * LANES)
    w_uq_ext = jnp.concatenate([nope, rope_p, rot_p], axis=1).astype(BF16)
    w_ukv3 = w_ukv.reshape(MLA_KV_RANK, H, MLA_NOPE + MLA_V)
    w_ukt = w_ukv3[:, :, :MLA_NOPE].reshape(MLA_KV_RANK, H * MLA_NOPE).T.astype(BF16)
    w_uv = w_ukv3[:, :, MLA_NOPE:].reshape(MLA_KV_RANK, H * MLA_V).astype(BF16)
    qv = jax.ShapeDtypeStruct((B, H, S, MLA_QK_PAD), BF16)
    bk = min(tm, MLA_KV_BLOCK)
    kt = jax.ShapeDtypeStruct((B, H, S // bk, MLA_QK_PAD, bk), BF16)
    tok = lambda w: pl.BlockSpec((1, tm, w), lambda b, s: (b, s, 0))
    hd = pl.BlockSpec((1, H, tm, MLA_QK_PAD), lambda b, s: (b, 0, s, 0))
    return pl.pallas_call(
        _mla_proj_kernel,
        out_shape=(qv, kt, qv),
        grid=(B, S // tm),
        in_specs=[tok(D), tok(LANES), tok(LANES), _const_spec(w_in_ext.shape), _const_spec((1, MLA_Q_RANK)),
                  _const_spec((1, MLA_KV_RANK)), _const_spec(w_uq_ext.shape), _const_spec(w_ukt.shape),
                  _const_spec(w_uv.shape)],
        out_specs=(hd, pl.BlockSpec((1, H, tm // bk, MLA_QK_PAD, bk), lambda b, s: (b, 0, s, 0, 0)), hd),
        compiler_params=_params(2),
        name="mla_proj",
    )(x, cos, sin, w_in_ext, _row(q_norm), _row(kv_norm), w_uq_ext, w_ukt, w_uv)


def _flash_kernel(q_ref, kt_ref, v_ref, o_ref, sa_ref, sb_ref, *, bq, bk):
    qi = pl.program_id(2)
    n_sub = bq // bk
    q = q_ref[0, 0]

    def scores(q_rows, j):
        return _dot(q_rows, kt_ref[0, 0, j])

    def update(s, j, m, acc, masked):
        start = pl.multiple_of(j * bk, bk)
        v = v_ref[0, 0, pl.ds(start, bk), :]
        if masked:
            row = lax.broadcasted_iota(jnp.int32, s.shape, 0)
            col = lax.broadcasted_iota(jnp.int32, s.shape, 1)
            s = jnp.where(col <= row, s, -jnp.inf)
        m_new = jnp.maximum(m, jnp.broadcast_to(jnp.max(s, axis=-1, keepdims=True), m.shape))
        p = jnp.exp2(s - jnp.concatenate([m_new] * (bk // LANES), axis=1))
        alpha = jnp.exp2(m - m_new)
        acc = jnp.concatenate([alpha] * (acc.shape[1] // LANES), axis=1) * acc + _dot(p.astype(BF16), v)
        return m_new, acc

    def pairs(n_pairs, base):
        def body(t, carry):
            m, acc = carry
            j = base + 2 * n_pairs * t
            for u in range(n_pairs):
                sb_ref[...] = scores(q, j + 2 * u + 1)
                m, acc = update(sa_ref[...], j + 2 * u, m, acc, False)
                sa_ref[...] = scores(q, j + 2 * u + 2)
                m, acc = update(sb_ref[...], j + 2 * u + 1, m, acc, False)
            return m, acc
        return body

    n_off = n_sub * qi
    m = jnp.full((bq, LANES), -jnp.inf, F32)
    acc = jnp.zeros((bq, 2 * MLA_V), F32)
    sa_ref[...] = scores(q, 0)
    n_long = n_off // (2 * FLASH_UNROLL_PAIRS)
    m, acc = lax.fori_loop(0, n_long, pairs(FLASH_UNROLL_PAIRS, 0), (m, acc))
    if n_sub % (2 * FLASH_UNROLL_PAIRS):
        done = 2 * FLASH_UNROLL_PAIRS * n_long
        m, acc = lax.fori_loop(0, (n_off - done) // 2, pairs(1, done), (m, acc))
    s = sa_ref[...]
    for t in range(n_sub):
        r0 = t * bk
        s_next = scores(q[r0 + bk:], n_off + t + 1) if t + 1 < n_sub else None
        m_t, acc_t = update(s, n_off + t, m[r0:], acc[r0:], True)
        m = m_t if t == 0 else jnp.concatenate([m[:r0], m_t], axis=0)
        acc = acc_t if t == 0 else jnp.concatenate([acc[:r0], acc_t], axis=0)
        s = s_next
    o_ref[0] = (acc[:, :MLA_V] / acc[:, MLA_V:MLA_V + 1]).astype(o_ref.dtype)


def _flash(q, kt, v, bq):
    B, H, S, _ = q.shape
    n_kv, _, bk = kt.shape[2:]
    assert (bq // bk) % 2 == 0 or S == bq
    return pl.pallas_call(
        functools.partial(_flash_kernel, bq=bq, bk=bk),
        out_shape=jax.ShapeDtypeStruct((B, S, H * MLA_V), BF16),
        grid=(B, H, S // bq),
        in_specs=[pl.BlockSpec((1, 1, bq, MLA_QK_PAD), lambda b, h, i: (b, h, i, 0)),
                  pl.BlockSpec((1, 1, n_kv, MLA_QK_PAD, bk), lambda b, h, i: (b, h, 0, 0, 0)),
                  pl.BlockSpec((1, 1, S, MLA_QK_PAD), lambda b, h, i: (b, h, 0, 0))],
        out_specs=pl.BlockSpec((1, bq, MLA_V), lambda b, h, i: (b, i, h)),
        scratch_shapes=[pltpu.VMEM((bq, bk), F32), pltpu.VMEM((bq, bk), F32)],
        compiler_params=_params(3),
        name="mla_flash",
    )(q, kt, v)


def _ffn_body(x, w_in_ref, cw_ref, cb_ref, w_out_ref, g_ref, b_ref, o_ref, halo_ref):
    tm = x.shape[0]

    @pl.when(pl.program_id(1) == 0)
    def _():
        halo_ref[...] = jnp.zeros_like(halo_ref)

    xb = x.astype(BF16)

    def conv(c0, c1):
        u = _dot(xb, w_in_ref[:, c0:c1])
        ext = jnp.concatenate([halo_ref[:, c0:c1], u], axis=0)
        halo_ref[:, c0:c1] = u[tm - SUBLANES:, :]
        s1 = pltpu.roll(ext, 1, 0)[SUBLANES:, :]
        s2 = pltpu.roll(ext, 2, 0)[SUBLANES:, :]
        w = cw_ref[:, c0:c1]
        return w[2:3, :] * u + w[1:2, :] * s1 + w[0:1, :] * s2 + cb_ref[:, c0:c1]

    acc = jnp.zeros((tm, D_MODEL), F32)
    for c0, c1 in FFN_CHUNKS:
        gate = conv(c0, c1)
        val = conv(FFN_DIM + c0, FFN_DIM + c1)
        act = (gate * _sigmoid(gate) * val).astype(BF16)
        acc = acc + _dot(act, w_out_ref[c0:c1, :])
    o_ref[0] = _layernorm(DEEPNORM_ALPHA * x + acc, g_ref[...], b_ref[...])


def _ffn_kernel(x_ref, *rest):
    _ffn_body(x_ref[0], *rest)


def _mix_ffn_kernel(a_ref, x_ref, w_o_ref, g1_ref, b1_ref, *rest):
    h = _dot(a_ref[0], w_o_ref[...])
    _ffn_body(_layernorm(DEEPNORM_ALPHA * x_ref[0] + h, g1_ref[...], b1_ref[...]), *rest)


def _ffn(x, w_in, conv_w, conv_b, w_out, g, b, tm, mixer=None):
    B, S, D = x.shape
    tok = lambda w: pl.BlockSpec((1, tm, w), lambda b_, s: (b_, s, 0))
    args = [x, w_in.astype(BF16), conv_w.astype(F32), _row(conv_b), w_out.astype(BF16), _row(g), _row(b)]
    specs = [tok(D), _const_spec((D, 2 * FFN_DIM)), _const_spec((FFN_CONV_WIDTH, 2 * FFN_DIM)),
             _const_spec((1, 2 * FFN_DIM)), _const_spec((FFN_DIM, D)), _const_spec((1, D)), _const_spec((1, D))]
    body = _ffn_kernel
    if mixer is not None:
        a, w_o, g1, b1 = mixer
        K = a.shape[-1]
        args = [a, x, w_o.astype(BF16), _row(g1), _row(b1)] + args[1:]
        specs = [tok(K), tok(D), _const_spec((K, D)), _const_spec((1, D)), _const_spec((1, D))] + specs[1:]
        body = _mix_ffn_kernel
    return pl.pallas_call(
        body,
        out_shape=jax.ShapeDtypeStruct((B, S, D), F32),
        grid=(B, S // tm),
        in_specs=specs,
        out_specs=tok(D),
        scratch_shapes=[pltpu.VMEM((SUBLANES, 2 * FFN_DIM), F32)],
        compiler_params=_params(2, FFN_VMEM_LIMIT),
        name="conv_ffn_ln",
    )(*args)


def _gla_proj_kernel(x_ref, w_in_ref, w_a2_ref, b_a_ref, q_ref, k_ref, v_ref, la_ref, gate_ref):
    hk = GLA_HEADS * GLA_DK
    hv = GLA_HEADS * GLA_DV
    proj = _dot(x_ref[0].astype(BF16), w_in_ref[...])
    q_ref[0] = (proj[:, :hk] * GLA_DK ** -0.5).astype(BF16)
    k_ref[0] = proj[:, hk:2 * hk].astype(BF16)
    v_ref[0] = proj[:, 2 * hk:2 * hk + hv].astype(BF16)
    r = proj[:, 2 * hk + hv:2 * hk + 2 * hv]
    gate_ref[0] = (r * _sigmoid(r)).astype(BF16)
    a_lr = proj[:, 2 * hk + 2 * hv:].astype(BF16)
    z = _dot(a_lr, w_a2_ref[...]) + b_a_ref[...]
    la_ref[0] = (jnp.minimum(z, 0.0) - jnp.log(1.0 + jnp.exp(-jnp.abs(z)))) * (1.0 / GLA_GATE_NORM)


def _gla_proj(x, w_in, w_a2, b_a, tm):
    B, S, D = x.shape
    hk = GLA_HEADS * GLA_DK
    hv = GLA_HEADS * GLA_DV
    a0 = 2 * hk + hv
    a_cols = jnp.pad(w_in[:, a0:a0 + GLA_GATE_RANK], ((0, 0), (0, LANES - GLA_GATE_RANK)))
    w_ext = jnp.concatenate([w_in[:, :a0], w_in[:, a0 + GLA_GATE_RANK:], a_cols], axis=1).astype(BF16)
    w_a2p = jnp.pad(w_a2, ((0, LANES - GLA_GATE_RANK), (0, 0))).astype(BF16)
    tok = lambda w: pl.BlockSpec((1, tm, w), lambda b_, s: (b_, s, 0))
    sd = lambda w, dt: jax.ShapeDtypeStruct((B, S, w), dt)
    return pl.pallas_call(
        _gla_proj_kernel,
        out_shape=(sd(hk, BF16), sd(hk, BF16), sd(hv, BF16), sd(hk, F32), sd(hv, BF16)),
        grid=(B, S // tm),
        in_specs=[tok(D), _const_spec(w_ext.shape), _const_spec(w_a2p.shape), _const_spec((1, hk))],
        out_specs=(tok(hk), tok(hk), tok(hv), tok(hk), tok(hv)),
        compiler_params=_params(2),
        name="gla_proj",
    )(x, w_ext, w_a2p, _row(b_a))


def _gla_tables(C):
    levels = [1 << e for e in range(int(math.log2(C)))][::-1]
    t = np.arange(C)
    u = np.arange(C)[None, :]
    blocks = []
    for s in levels:
        ref = (t // (2 * s)) * (2 * s) + s - 1
        upper = ((t // s) % 2) == 1
        q_role = (u > ref[:, None]) & (u <= t[:, None])
        k_role = (u > t[:, None]) & (u <= ref[:, None])
        blocks.append(np.where(upper[:, None], q_role, k_role))
    blocks.append(u <= t[:, None])
    sums = np.concatenate(blocks, axis=0).astype(np.float32)
    x = t[:, None] ^ t[None, :]
    lvl = np.full((C, C), -1, np.int32)
    for n, s in enumerate(levels):
        lvl = np.where((t[:, None] > t[None, :]) & (x >= s) & (x < 2 * s), n, lvl)
    lvl = np.where(t[:, None] == t[None, :], len(levels), lvl)
    return len(levels), sums, lvl.astype(np.int32)


def _gla_core_kernel(q_ref, k_ref, v_ref, la_ref, gate_ref, norm_ref, sums_ref, lvl_ref, o_ref, state_ref,
                     *, n_levels):
    C = GLA_CHUNK
    DK, DV = GLA_DK, GLA_DV

    @pl.when(pl.program_id(1) == 0)
    def _():
        state_ref[...] = jnp.zeros_like(state_ref)

    lvl = lvl_ref[...]
    for h in range(GLA_HEADS):
        q = q_ref[0, :, h * DK:(h + 1) * DK].astype(F32)
        k = k_ref[0, :, h * DK:(h + 1) * DK].astype(F32)
        v = v_ref[0, :, h * DV:(h + 1) * DV]
        la = la_ref[0, :, h * DK:(h + 1) * DK]
        hi = la.astype(BF16)
        lo = (la - hi.astype(F32)).astype(BF16)
        e2 = _dot(sums_ref[...], jnp.concatenate([hi, lo], axis=1))
        e = e2[:, :DK] + e2[:, DK:]

        a = jnp.where(lvl == n_levels, jnp.sum(q * k, axis=-1, keepdims=True), 0.0)
        for n in range(n_levels):
            w = jnp.exp(e[n * C:(n + 1) * C, :])
            p = _dot_nt((q * w).astype(BF16), (k * w).astype(BF16))
            a = jnp.where(lvl == n, p, a)
        e_q = e[n_levels * C:(n_levels + 1) * C, :]
        e_k = e_q[C - 1:C, :] - e_q
        st = state_ref[h]
        o = _dot_nt((q * jnp.exp(e_q)).astype(BF16), st.astype(BF16)) + _dot(a.astype(BF16), v)
        k_dec = (k * jnp.exp(e_k)).astype(BF16)
        state_ref[h] = jnp.exp(e_q[C - 1:C, :]) * st + _dot_tn(v, k_dec)
        o = _rmsnorm(o, norm_ref[...])
        o_ref[0, :, h * DV:(h + 1) * DV] = (o * gate_ref[0, :, h * DV:(h + 1) * DV].astype(F32)).astype(BF16)


def _gla_core(q, k, v, la, gate, out_norm):
    B, S, hk = q.shape
    hv = v.shape[-1]
    C = GLA_CHUNK
    n_levels, sums, lvl = _gla_tables(C)
    tok = lambda w: pl.BlockSpec((1, C, w), lambda b_, s: (b_, s, 0))
    return pl.pallas_call(
        functools.partial(_gla_core_kernel, n_levels=n_levels),
        out_shape=jax.ShapeDtypeStruct((B, S, hv), BF16),
        grid=(B, S // C),
        in_specs=[tok(hk), tok(hk), tok(hv), tok(hk), tok(hv), _const_spec((1, GLA_DV)),
                  _const_spec(sums.shape), _const_spec(lvl.shape)],
        out_specs=tok(hv),
        scratch_shapes=[pltpu.VMEM((GLA_HEADS, GLA_DV, GLA_DK), F32)],
        compiler_params=_params(2),
        name="gla_core",
    )(q, k, v, la, gate, _row(out_norm), jnp.asarray(sums, BF16), jnp.asarray(lvl))


def _conformer_kernel(x_ref, w_in_ref, b_in_ref, dw_ref, dwb_ref, lng_ref, lnb_ref, w_o_ref, b_o_ref,
                      g1_ref, b1_ref, o_ref, hist_ref, y_ref, *, tm):
    D = D_MODEL
    HALO = CONV_HALO

    @pl.when(pl.program_id(1) == 0)
    def _():
        hist_ref[0:HALO, :] = jnp.zeros((HALO, D), F32)

    x = x_ref[0]
    h = _dot(x.astype(BF16), w_in_ref[...]) + b_in_ref[...]
    hist_ref[HALO:HALO + tm, :] = h[:, :D] * _sigmoid(h[:, D:])

    n_m = (CONV_WIDTH + SUBLANES - 1) // SUBLANES
    for c0 in range(0, D, LANES):
        y = None
        for r in range(SUBLANES):
            z = None
            for m in range(n_m):
                d = SUBLANES * m + r
                if d >= CONV_WIDTH:
                    continue
                start = HALO - SUBLANES - SUBLANES * m
                tap = CONV_WIDTH - 1 - d
                term = dw_ref[tap:tap + 1, c0:c0 + LANES] * hist_ref[start:start + tm + SUBLANES, c0:c0 + LANES]
                z = term if z is None else z + term
            if r:
                z = pltpu.roll(z, r, 0)
            y = z if y is None else y + z
        y_ref[:, c0:c0 + LANES] = y[SUBLANES:, :] + dwb_ref[:, c0:c0 + LANES]
    hist_ref[0:HALO, :] = hist_ref[tm:tm + HALO, :]

    t = _layernorm(y_ref[...], lng_ref[...], lnb_ref[...])
    t = (t * _sigmoid(t)).astype(BF16)
    out = _dot(t, w_o_ref[...]) + b_o_ref[...]
    o_ref[0] = _layernorm(DEEPNORM_ALPHA * x + out, g1_ref[...], b1_ref[...])


def _conformer(x, w_in, b_in, dw, dw_b, ln_g, ln_b, w_o, b_o, g1, b1, tm):
    B, S, D = x.shape
    return pl.pallas_call(
        functools.partial(_conformer_kernel, tm=tm),
        out_shape=jax.ShapeDtypeStruct((B, S, D), F32),
        grid=(B, S // tm),
        in_specs=[pl.BlockSpec((1, tm, D), lambda b_, s: (b_, s, 0)),
                  _const_spec((D, 2 * D)), _const_spec((1, 2 * D)), _const_spec((CONV_WIDTH, D)),
                  _const_spec((1, D)), _const_spec((1, D)), _const_spec((1, D)), _const_spec((D, D)),
                  _const_spec((1, D)), _const_spec((1, D)), _const_spec((1, D))],
        out_specs=pl.BlockSpec((1, tm, D), lambda b_, s: (b_, s, 0)),
        scratch_shapes=[pltpu.VMEM((CONV_HALO + tm, D), F32), pltpu.VMEM((tm, D), F32)],
        compiler_params=_params(2),
        name="conformer_ln",
    )(x, w_in.astype(BF16), _row(b_in), dw.astype(F32), _row(dw_b), _row(ln_g), _row(ln_b),
      w_o.astype(BF16), _row(b_o), _row(g1), _row(b1))


def _tile(S, want):
    t = min(want, S)
    assert S % t == 0
    return t


def _mla_mix(x, rope, w_in, q_norm, kv_norm, w_uq, w_ukv):
    S = x.shape[1]
    q, kt, v = _mla_proj(x, rope[0], rope[1], w_in, q_norm, kv_norm, w_uq, w_ukv, _tile(S, 2 * TOKEN_TILE))
    return _flash(q, kt, v, _tile(S, MLA_Q_BLOCK))


def _gla_mix(x, w_in, w_a2, b_a, out_norm):
    S = x.shape[1]
    q, k, v, la, gate = _gla_proj(x, w_in, w_a2, b_a, _tile(S, 2 * TOKEN_TILE))
    return _gla_core(q, k, v, la, gate, out_norm)


def kernel(x, positions, l0_mla_w_in, l0_mla_q_norm, l0_mla_kv_norm, l0_mla_w_uq, l0_mla_w_ukv, l0_mla_w_o, l0_ln1_g, l0_ln1_b, l0_ffn_w_in, l0_ffn_conv, l0_ffn_conv_b, l0_ffn_w_out, l0_ln2_g, l0_ln2_b, l1_gla_w_in, l1_gla_w_a2, l1_gla_b_a, l1_gla_out_norm, l1_gla_w_o, l1_ln1_g, l1_ln1_b, l1_ffn_w_in, l1_ffn_conv, l1_ffn_conv_b, l1_ffn_w_out, l1_ln2_g, l1_ln2_b, l2_conv_w_in, l2_conv_b_in, l2_conv_dw, l2_conv_dw_b, l2_conv_ln_g, l2_conv_ln_b, l2_conv_w_o, l2_conv_b_o, l2_ln1_g, l2_ln1_b, l2_ffn_w_in, l2_ffn_conv, l2_ffn_conv_b, l2_ffn_w_out, l2_ln2_g, l2_ln2_b, l3_mla_w_in, l3_mla_q_norm, l3_mla_kv_norm, l3_mla_w_uq, l3_mla_w_ukv, l3_mla_w_o, l3_ln1_g, l3_ln1_b, l3_ffn_w_in, l3_ffn_conv, l3_ffn_conv_b, l3_ffn_w_out, l3_ln2_g, l3_ln2_b):
    S = x.shape[1]
    assert S % GLA_CHUNK == 0 or S < GLA_CHUNK
    tm = _tile(S, FFN_TILE)
    rope = _rope_table(positions, _tile(S, TOKEN_TILE))

    o = _mla_mix(x, rope, l0_mla_w_in, l0_mla_q_norm, l0_mla_kv_norm, l0_mla_w_uq, l0_mla_w_ukv)
    x = _ffn(x, l0_ffn_w_in, l0_ffn_conv, l0_ffn_conv_b, l0_ffn_w_out, l0_ln2_g, l0_ln2_b, tm,
             mixer=(o, l0_mla_w_o, l0_ln1_g, l0_ln1_b))

    o = _gla_mix(x, l1_gla_w_in, l1_gla_w_a2, l1_gla_b_a, l1_gla_out_norm)
    x = _ffn(x, l1_ffn_w_in, l1_ffn_conv, l1_ffn_conv_b, l1_ffn_w_out, l1_ln2_g, l1_ln2_b, tm,
             mixer=(o, l1_gla_w_o, l1_ln1_g, l1_ln1_b))

    x = _conformer(x, l2_conv_w_in, l2_conv_b_in, l2_conv_dw, l2_conv_dw_b, l2_conv_ln_g, l2_conv_ln_b,
                   l2_conv_w_o, l2_conv_b_o, l2_ln1_g, l2_ln1_b, _tile(S, CONFORMER_TILE))
    x = _ffn(x, l2_ffn_w_in, l2_ffn_conv, l2_ffn_conv_b, l2_ffn_w_out, l2_ln2_g, l2_ln2_b, tm)

    o = _mla_mix(x, rope, l3_mla_w_in, l3_mla_q_norm, l3_mla_kv_norm, l3_mla_w_uq, l3_mla_w_ukv)
    x = _ffn(x, l3_ffn_w_in, l3_ffn_conv, l3_ffn_conv_b, l3_ffn_w_out, l3_ln2_g, l3_ln2_b, tm,
             mixer=(o, l3_mla_w_o, l3_ln1_g, l3_ln1_b))
    return x
```
